```python
import jax
import jax.numpy as jnp
from jax import lax
import numpy as np

D_MODEL = 1024
BATCH = 8
SEQ = 4096
DEPTH = 4

GRID_W = 64
CTX_LEN = 256
N_MIXERS = 3
N_LAYERS_SWA = (DEPTH + 2) // 3
N_LAYERS_MLA = (DEPTH + 1) // 3
N_LAYERS_LRU = DEPTH // 3
N_MOD = 6

HEAD_DIM = 64
SWA_HEADS = D_MODEL // HEAD_DIM
SWA_KV_HEADS = SWA_HEADS // 4
SWA_GROUP = SWA_HEADS // SWA_KV_HEADS
WINDOW = 128
BLOCK = 128

MLA_HEADS = D_MODEL // 64
MLA_NOPE = 64
MLA_ROPE = 32
MLA_QK = MLA_NOPE + MLA_ROPE
MLA_V = 64
Q_LORA = 384
KV_LORA = 256

LRU_WIDTH = D_MODEL
LRU_BLOCKS = 4
LRU_BW = LRU_WIDTH // LRU_BLOCKS
LRU_CONV = 4
LRU_C = 8.0

D_FF = 2816
FFN_CONV = 3

ROPE_BASE = 10000.0
EPS = 1e-6
NEG_INF = -1e30

kernel_name = "hybrid_dit_swa_mla_rglru"


def rmsnorm(x, gain):
    xf = x.astype(jnp.float32)
    y = xf * lax.rsqrt(jnp.mean(xf * xf, axis=-1, keepdims=True) + EPS)
    return (y * gain.astype(jnp.float32)).astype(x.dtype)


def modulate(h, shift, scale):
    return h * (1 + scale) + shift


def depthwise_conv(x, w, b, pad_left):
    k = w.shape[0]
    y = lax.conv_general_dilated(
        x, w[:, None, :].astype(x.dtype), (1,), [(pad_left, k - 1 - pad_left)],
        dimension_numbers=("NWC", "WIO", "NWC"), feature_group_count=x.shape[-1])
    return y + b.astype(x.dtype)


def axial_rope_tables(n_tokens, rot_dim):
    rows = n_tokens // GRID_W
    row = jnp.repeat(jnp.arange(rows, dtype=jnp.float32), GRID_W)
    col = jnp.tile(jnp.arange(GRID_W, dtype=jnp.float32), rows)
    n_freq = rot_dim // 4
    inv = ROPE_BASE ** (-jnp.arange(n_freq, dtype=jnp.float32) / n_freq)
    ang_r = row[:, None] * inv
    ang_c = col[:, None] * inv
    return (jnp.cos(ang_r), jnp.sin(ang_r), jnp.cos(ang_c), jnp.sin(ang_c))


def _rotate_half(x, cos, sin):
    n = cos.shape[-1]
    x1, x2 = x[..., :n], x[..., n:]
    cos, sin = cos[:, None, :], sin[:, None, :]
    return jnp.concatenate([x1 * cos - x2 * sin, x2 * cos + x1 * sin], axis=-1)


def apply_axial_rope(x, tables):
    cos_r, sin_r, cos_c, sin_c = tables
    half = x.shape[-1] // 2
    xf = x.astype(jnp.float32)
    out = jnp.concatenate([_rotate_half(xf[..., :half], cos_r, sin_r),
                           _rotate_half(xf[..., half:], cos_c, sin_c)], axis=-1)
    return out.astype(x.dtype)


def attend(q, k, v, mask=None, sink=None):
    s = jnp.einsum("bqhgd,bkhd->bhgqk", q, k, preferred_element_type=jnp.float32) * (q.shape[-1] ** -0.5)
    if mask is not None:
        s = jnp.where(mask, s, NEG_INF)
    m = jnp.max(s, axis=-1)
    if sink is not None:
        sk = jnp.broadcast_to(sink.astype(jnp.float32)[None, :, :, None], m.shape)
        m = jnp.maximum(m, sk)
    p = jnp.exp(s - m[..., None])
    denom = jnp.sum(p, axis=-1)
    if sink is not None:
        denom = denom + jnp.exp(sk - m)
    o = jnp.einsum("bhgqk,bkhd->bqhgd", p, v.astype(jnp.float32))
    o = o / jnp.transpose(denom, (0, 3, 1, 2))[..., None]
    return o.astype(v.dtype)


def window_gqa_mixer(h_lat, h_ctx, w_qkv, q_gain, k_gain, sink, w_o, rope, with_ctx_out):
    bsz, seq, _ = h_lat.shape
    n_blocks = seq // BLOCK
    sink_hg = sink.reshape(SWA_KV_HEADS, SWA_GROUP)

    def project(h):
        n = h.shape[1]
        q, k, v = jnp.split(h @ w_qkv, [SWA_HEADS * HEAD_DIM, (SWA_HEADS + SWA_KV_HEADS) * HEAD_DIM], axis=-1)
        q = rmsnorm(q.reshape(bsz, n, SWA_HEADS, HEAD_DIM), q_gain)
        k = rmsnorm(k.reshape(bsz, n, SWA_KV_HEADS, HEAD_DIM), k_gain)
        return q, k, v.reshape(bsz, n, SWA_KV_HEADS, HEAD_DIM)

    qc, kc, vc = project(h_ctx)
    q, k, v = project(h_lat)
    q = apply_axial_rope(q, rope).reshape(bsz, seq, SWA_KV_HEADS, SWA_GROUP, HEAD_DIM)
    k = apply_axial_rope(k, rope)
    pad = ((0, 0), (WINDOW, WINDOW), (0, 0), (0, 0))
    kp, vp = jnp.pad(k, pad), jnp.pad(v, pad)
    span = BLOCK + 2 * WINDOW
    q_idx = jnp.arange(BLOCK)[:, None]
    k_idx = jnp.arange(span)[None, :]
    ctx_mask = jnp.ones((BLOCK, kc.shape[1]), dtype=bool)

    def block(b):
        start = b * BLOCK
        qb = lax.dynamic_slice_in_dim(q, start, BLOCK, axis=1)
        kb = lax.dynamic_slice_in_dim(kp, start, span, axis=1)
        vb = lax.dynamic_slice_in_dim(vp, start, span, axis=1)
        key_pos = start - WINDOW + k_idx
        band = (jnp.abs(q_idx + WINDOW - k_idx) <= WINDOW) & (key_pos >= 0) & (key_pos < seq)
        mask = jnp.concatenate([band, ctx_mask], axis=1)
        return attend(qb, jnp.concatenate([kb, kc], axis=1), jnp.concatenate([vb, vc], axis=1), mask, sink_hg)

    o = lax.map(block, jnp.arange(n_blocks))
    o = jnp.transpose(o, (1, 0, 2, 3, 4, 5)).reshape(bsz, seq, SWA_HEADS * HEAD_DIM)
    y_lat = o @ w_o
    y_ctx = None
    if with_ctx_out:
        n_ctx = h_ctx.shape[1]
        oc = attend(qc.reshape(bsz, n_ctx, SWA_KV_HEADS, SWA_GROUP, HEAD_DIM), kc, vc, None, sink_hg)
        y_ctx = oc.reshape(bsz, n_ctx, SWA_HEADS * HEAD_DIM) @ w_o
    return y_lat, y_ctx


def mla_mixer(h_lat, h_ctx, w_down, q_lora_gain, w_uq, kv_lora_gain, w_uk, w_uv, q_gain, k_gain, w_o,
              rope, with_ctx_out):
    bsz, seq, _ = h_lat.shape
    n_blocks = seq // BLOCK

    def project(h, tables):
        n = h.shape[1]
        cq, ckv, k_rope = jnp.split(h @ w_down, [Q_LORA, Q_LORA + KV_LORA], axis=-1)
        q = (rmsnorm(cq, q_lora_gain) @ w_uq).reshape(bsz, n, MLA_HEADS, MLA_QK)
        ckv = rmsnorm(ckv, kv_lora_gain)
        k_nope = (ckv @ w_uk).reshape(bsz, n, MLA_HEADS, MLA_NOPE)
        v = (ckv @ w_uv).reshape(bsz, n, MLA_HEADS, MLA_V)
        k_rope = jnp.broadcast_to(k_rope[:, :, None, :], (bsz, n, MLA_HEADS, MLA_ROPE))
        q = rmsnorm(q, q_gain)
        k = rmsnorm(jnp.concatenate([k_nope, k_rope], axis=-1), k_gain)
        if tables is not None:
            q = jnp.concatenate([q[..., :MLA_NOPE], apply_axial_rope(q[..., MLA_NOPE:], tables)], axis=-1)
            k = jnp.concatenate([k[..., :MLA_NOPE], apply_axial_rope(k[..., MLA_NOPE:], tables)], axis=-1)
        return q[:, :, :, None, :], k, v

    qc, kc, vc = project(h_ctx, None)
    q, k, v = project(h_lat, rope)
    k_all = jnp.concatenate([k, kc], axis=1)
    v_all = jnp.concatenate([v, vc], axis=1)

    def block(b):
        qb = lax.dynamic_slice_in_dim(q, b * BLOCK, BLOCK, axis=1)
        return attend(qb, k_all, v_all)

    o = lax.map(block, jnp.arange(n_blocks))
    o = jnp.transpose(o, (1, 0, 2, 3, 4, 5)).reshape(bsz, seq, MLA_HEADS * MLA_V)
    y_lat = o @ w_o
    y_ctx = None
    if with_ctx_out:
        oc = attend(qc, kc, vc)
        y_ctx = oc.reshape(bsz, h_ctx.shape[1], MLA_HEADS * MLA_V) @ w_o
    return y_lat, y_ctx


def rglru_coeffs(x, gate_w, gate_b, lam):
    bsz, n, _ = x.shape
    xb = x.reshape(bsz, n, LRU_BLOCKS, LRU_BW)
    g = jnp.einsum("blnc,kncd->kblnd", xb, gate_w).reshape(2, bsz, n, LRU_WIDTH) + gate_b[:, None, None, :]
    g = jax.nn.sigmoid(g.astype(jnp.float32))
    r, i = g[0], g[1]
    log_a = -LRU_C * r * jax.nn.softplus(-lam.astype(jnp.float32))
    a = jnp.exp(log_a)
    b = jnp.sqrt(-jnp.expm1(2.0 * log_a)) * (i * x.astype(jnp.float32))
    return a, b


def linear_scan(a, b, h0, reverse):
    first = -1 if reverse else 0
    b = b.at[:, first].add(a[:, first] * h0)

    def combine(left, right):
        a_l, b_l = left
        a_r, b_r = right
        return a_l * a_r, a_r * b_l + b_r

    return lax.associative_scan(combine, (a, b), reverse=reverse, axis=1)[1]


def rglru_mixer(h_lat, h_ctx, w_in, conv_w, conv_b, gate_w, gate_b, lam, w_out, with_ctx_out):
    def branches(h):
        gate, xr = jnp.split(h @ w_in, 2, axis=-1)
        return jax.nn.gelu(gate, approximate=True), depthwise_conv(xr, conv_w, conv_b, LRU_CONV // 2)

    g_ctx, x_ctx = branches(h_ctx)
    g_lat, x_lat = branches(h_lat)
    h0 = jnp.zeros((h_lat.shape[0], LRU_WIDTH), jnp.float32)
    lat_states, ctx_states = [], []
    for d, reverse in enumerate((False, True)):
        a_c, b_c = rglru_coeffs(x_ctx, gate_w[d], gate_b[d], lam[d])
        s_ctx = linear_scan(a_c, b_c, h0, reverse)
        h_end = s_ctx[:, 0] if reverse else s_ctx[:, -1]
        a_l, b_l = rglru_coeffs(x_lat, gate_w[d], gate_b[d], lam[d])
        lat_states.append(linear_scan(a_l, b_l, h_end, reverse))
        ctx_states.append(s_ctx)
    r_lat = lat_states[0] + lat_states[1]
    y_lat = (g_lat * r_lat.astype(g_lat.dtype)) @ w_out
    y_ctx = None
    if with_ctx_out:
        r_ctx = ctx_states[0] + ctx_states[1]
        y_ctx = (g_ctx * r_ctx.astype(g_ctx.dtype)) @ w_out
    return y_lat, y_ctx


def conv_ffn(h, w_up, conv_w, conv_b, w_down):
    gate, val = jnp.split(h @ w_up, 2, axis=-1)
    gate = depthwise_conv(gate, conv_w, conv_b, FFN_CONV // 2)
    return (jax.nn.silu(gate) * val) @ w_down


def setup_inputs(seed: int = 0) -> dict:
    key = jax.random.key(seed)
    keys = iter(jax.random.split(key, 48))

    def nrm(shape, scale):
        return jax.random.normal(next(keys), shape, jnp.float32) * scale

    def gain(shape):
        return 1.0 + nrm(shape, 0.05)

    D = D_MODEL
    u = jax.random.uniform(next(keys), (N_LAYERS_LRU, 2, LRU_WIDTH), jnp.float32, minval=0.9, maxval=0.999)
    s = u ** (1.0 / LRU_C)
    lam = jnp.log(s) - jnp.log1p(-s)
    return {
        "x": nrm((BATCH, SEQ, D), 1.0),
        "c": nrm((BATCH, D), 1.0),
        "ctx": nrm((BATCH, CTX_LEN, D), 1.0),
        "c_ctx": nrm((D,), 1.0),
        "norm1": gain((DEPTH, D)),
        "norm2": gain((DEPTH, D)),
        "mod_w": nrm((DEPTH, D, N_MOD * D), 0.5 * D ** -0.5),
        "mod_b": nrm((DEPTH, N_MOD * D), 0.02),
        "swa_w_qkv": nrm((N_LAYERS_SWA, D, (SWA_HEADS + 2 * SWA_KV_HEADS) * HEAD_DIM), D ** -0.5),
        "swa_q_gain": gain((N_LAYERS_SWA, HEAD_DIM)),
        "swa_k_gain": gain((N_LAYERS_SWA, HEAD_DIM)),
        "swa_sink": nrm((N_LAYERS_SWA, SWA_HEADS), 1.0),
        "swa_w_o": nrm((N_LAYERS_SWA, SWA_HEADS * HEAD_DIM, D), (SWA_HEADS * HEAD_DIM) ** -0.5),
        "mla_w_down": nrm((N_LAYERS_MLA, D, Q_LORA + KV_LORA + MLA_ROPE), D ** -0.5),
        "mla_q_lora_gain": gain((N_LAYERS_MLA, Q_LORA)),
        "mla_w_uq": nrm((N_LAYERS_MLA, Q_LORA, MLA_HEADS * MLA_QK), Q_LORA ** -0.5),
        "mla_kv_lora_gain": gain((N_LAYERS_MLA, KV_LORA)),
        "mla_w_uk": nrm((N_LAYERS_MLA, KV_LORA, MLA_HEADS * MLA_NOPE), KV_LORA ** -0.5),
        "mla_w_uv": nrm((N_LAYERS_MLA, KV_LORA, MLA_HEADS * MLA_V), KV_LORA ** -0.5),
        "mla_q_gain": gain((N_LAYERS_MLA, MLA_QK)),
        "mla_k_gain": gain((N_LAYERS_MLA, MLA_QK)),
        "mla_w_o": nrm((N_LAYERS_MLA, MLA_HEADS * MLA_V, D), (MLA_HEADS * MLA_V) ** -0.5),
        "lru_w_in": nrm((N_LAYERS_LRU, D, 2 * LRU_WIDTH), D ** -0.5),
        "lru_conv_w": nrm((N_LAYERS_LRU, LRU_CONV, LRU_WIDTH), LRU_CONV ** -0.5),
        "lru_conv_b": nrm((N_LAYERS_LRU, LRU_WIDTH), 0.01),
        "lru_gate_w": nrm((N_LAYERS_LRU, 2, 2, LRU_BLOCKS, LRU_BW, LRU_BW), LRU_BW ** -0.5),
        "lru_gate_b": nrm((N_LAYERS_LRU, 2, 2, LRU_WIDTH), 0.01),
        "lru_lam": lam,
        "lru_w_out": nrm((N_LAYERS_LRU, LRU_WIDTH, D), LRU_WIDTH ** -0.5),
        "ffn_w_up": nrm((DEPTH, D, 2 * D_FF), D ** -0.5),
        "ffn_conv_w": nrm((DEPTH, FFN_CONV, D_FF), FFN_CONV ** -0.5),
        "ffn_conv_b": nrm((DEPTH, D_FF), 0.01),
        "ffn_w_down": nrm((DEPTH, D_FF, D), D_FF ** -0.5),
    }


def reference(x, c, ctx, c_ctx, norm1, norm2, mod_w, mod_b,
              swa_w_qkv, swa_q_gain, swa_k_gain, swa_sink, swa_w_o,
              mla_w_down, mla_q_lora_gain, mla_w_uq, mla_kv_lora_gain, mla_w_uk, mla_w_uv,
              mla_q_gain, mla_k_gain, mla_w_o,
              lru_w_in, lru_conv_w, lru_conv_b, lru_gate_w, lru_gate_b, lru_lam, lru_w_out,
              ffn_w_up, ffn_conv_w, ffn_conv_b, ffn_w_down):
    seq = x.shape[1]
    rope_swa = axial_rope_tables(seq, HEAD_DIM)
    rope_mla = axial_rope_tables(seq, MLA_ROPE)
    cond_lat = jax.nn.silu(c)
    cond_ctx = jax.nn.silu(c_ctx)
    for layer in range(DEPTH):
        kind, idx = layer % N_MIXERS, layer // N_MIXERS
        with_ctx_out = layer < DEPTH - 1
        mod_l = jnp.split(cond_lat @ mod_w[layer] + mod_b[layer], N_MOD, axis=-1)
        mod_c = jnp.split(cond_ctx @ mod_w[layer] + mod_b[layer], N_MOD, axis=-1)
        sh1, sc1, g1, sh2, sc2, g2 = [m[:, None, :] for m in mod_l]
        csh1, csc1, cg1, csh2, csc2, cg2 = mod_c
        h_lat = modulate(rmsnorm(x, norm1[layer]), sh1, sc1)
        h_ctx = modulate(rmsnorm(ctx, norm1[layer]), csh1, csc1)
        if kind == 0:
            y_lat, y_ctx = window_gqa_mixer(h_lat, h_ctx, swa_w_qkv[idx], swa_q_gain[idx], swa_k_gain[idx],
                                            swa_sink[idx], swa_w_o[idx], rope_swa, with_ctx_out)
        elif kind == 1:
            y_lat, y_ctx = mla_mixer(h_lat, h_ctx, mla_w_down[idx], mla_q_lora_gain[idx], mla_w_uq[idx],
                                     mla_kv_lora_gain[idx], mla_w_uk[idx], mla_w_uv[idx], mla_q_gain[idx],
                                     mla_k_gain[idx], mla_w_o[idx], rope_mla, with_ctx_out)
        else:
            y_lat, y_ctx = rglru_mixer(h_lat, h_ctx, lru_w_in[idx], lru_conv_w[idx], lru_conv_b[idx],
                                       lru_gate_w[idx], lru_gate_b[idx], lru_lam[idx], lru_w_out[idx],
                                       with_ctx_out)
        x = x + g1 * y_lat
        x = x + g2 * conv_ffn(modulate(rmsnorm(x, norm2[layer]), sh2, sc2),
                              ffn_w_up[layer], ffn_conv_w[layer], ffn_conv_b[layer], ffn_w_down[layer])
        if with_ctx_out:
            ctx = ctx + cg1 * y_ctx
            ctx = ctx + cg2 * conv_ffn(modulate(rmsnorm(ctx, norm2[layer]), csh2, csc2),
                                       ffn_w_up[layer], ffn_conv_w[layer], ffn_conv_b[layer], ffn_w_down[layer])
    return x
```

```python
import functools

import jax
import jax.numpy as jnp
from jax import lax
from jax.experimental import pallas as pl
from jax.experimental.pallas import tpu as pltpu

F32 = jnp.float32
BF16 = jnp.bfloat16

GRID_W = 64
HEAD_DIM = 64
SWA_HEADS = 16
SWA_KV_HEADS = 4
WINDOW = 128
MLA_HEADS = 16
MLA_NOPE = 64
MLA_ROPE = 32
MLA_QK = MLA_NOPE + MLA_ROPE
Q_LORA = 384
KV_LORA = 256
LRU_BLOCKS = 4
LRU_BW = 256
LRU_C = 8.0
D_FF = 2816
ROPE_BASE = 10000.0
EPS = 1e-6
NEG_INF = -1e30
LOG2E = 1.4426950408889634

LANES = 128
FF_CHUNK = 256
HALO = 8
VMEM_LIMIT = 56 * 1024 * 1024


def _cparams(*sem):
    return pltpu.CompilerParams(dimension_semantics=sem, vmem_limit_bytes=VMEM_LIMIT)


def _const(shape):
    nd = len(shape)
    return pl.BlockSpec(shape, lambda *_: (0,) * nd, pipeline_mode=pl.Buffered(1))


def _rows(tm, d):
    return pl.BlockSpec((1, tm, d), lambda b, i: (b, i, 0))


def _vec(d):
    return pl.BlockSpec((1, 1, d), lambda b, i: (b, 0, 0))


def _dot(a, b):
    return jnp.dot(a, b, preferred_element_type=F32)


def _dot_t(a, b):
    return lax.dot_general(a, b, (((1,), (1,)), ((), ())), preferred_element_type=F32)


def _modnorm(x, gain, shift, scale):
    ms = jnp.mean(x * x, axis=-1, keepdims=True)
    y = x * lax.rsqrt(ms + EPS) * gain
    return y * (1.0 + scale) + shift


def _rope128(x, cos, sin, half):
    lane = lax.broadcasted_iota(jnp.int32, x.shape, 1)
    up = pltpu.roll(x, LANES - half, 1)
    dn = pltpu.roll(x, half, 1)
    return x * cos + jnp.where((lane & half) == 0, up, dn) * sin


def _mod_body(c_ref, w_ref, b_ref, o_ref):
    c = c_ref[...]
    cs = (c * jax.nn.sigmoid(c)).astype(BF16)
    o_ref[0] = _dot(cs, w_ref[0].astype(BF16)) + b_ref[0]


def _mod_all(cond, mod_w, mod_b):
    depth, d, n = mod_w.shape
    tn = 1536
    rows = cond.shape[0]
    return pl.pallas_call(
        _mod_body,
        grid=(depth, n // tn),
        in_specs=[pl.BlockSpec((rows, d), lambda l, j: (0, 0)),
                  pl.BlockSpec((1, d, tn), lambda l, j: (l, 0, j)),
                  pl.BlockSpec((1, 1, tn), lambda l, j: (l, 0, j))],
        out_specs=pl.BlockSpec((1, rows, tn), lambda l, j: (l, 0, j)),
        out_shape=jax.ShapeDtypeStruct((depth, rows, n), F32),
        compiler_params=_cparams("arbitrary", "arbitrary"),
        name="mod",
    )(cond, mod_w, mod_b.reshape(depth, 1, n))


def _swa_proj_body(rope, x_ref, gain_ref, sh_ref, sc_ref, wq_ref, wk_ref, wv_ref, bd_ref,
                   gq_ref, gk_ref, *rest):
    if rope:
        cos_ref, sin_ref, q_out, k_out, v_out = rest
    else:
        q_out, k_out, v_out = rest
    h = _modnorm(x_ref[0], gain_ref[...], sh_ref[0], sc_ref[0]).astype(BF16)
    q = _dot(h, wq_ref[...])
    k = _dot(h, wk_ref[...])
    v_out[0] = _dot(h, wv_ref[...]).astype(BF16)
    ssq = _dot((q * q).astype(BF16), bd_ref[...])
    q = q * lax.rsqrt(ssq * (1.0 / HEAD_DIM) + EPS) * gq_ref[...]
    ks = []
    for g in range(SWA_KV_HEADS):
        kg = k[:, g * LANES:(g + 1) * LANES]
        ks.append(kg * lax.rsqrt(jnp.mean(kg * kg, axis=-1, keepdims=True) + EPS))
    k = jnp.concatenate(ks, axis=1) * gk_ref[...]
    if rope:
        cos, sin = cos_ref[...], sin_ref[...]
        q = jnp.concatenate([_rope128(q[:, j * LANES:(j + 1) * LANES], cos, sin, 16)
                             for j in range(q.shape[1] // LANES)], axis=1)
        k = jnp.concatenate([_rope128(k[:, j * LANES:(j + 1) * LANES], cos, sin, 16)
                             for j in range(k.shape[1] // LANES)], axis=1)
    q_out[0] = q.astype(BF16)
    k_out[0] = k.astype(BF16)


def _swa_proj(x, gain, sh, sc, w, tables, tm):
    b, l, d = x.shape
    wq, wk, wv, bd, gq, gk = w
    rope = tables is not None
    ins = [x, gain, sh, sc, wq, wk, wv, bd, gq, gk]
    specs = [_rows(tm, d), _const(gain.shape), _vec(d), _vec(d), _const(wq.shape), _const(wk.shape),
             _const(wv.shape), _const(bd.shape), _const(gq.shape), _const(gk.shape)]
    if rope:
        ins += list(tables)
        specs += [pl.BlockSpec((tm, LANES), lambda bb, i: (i, 0))] * 2
    nq, nk = wq.shape[1], wk.shape[1]
    return pl.pallas_call(
        functools.partial(_swa_proj_body, rope),
        grid=(b, l // tm),
        in_specs=specs,
        out_specs=[_rows(tm, nq), _rows(tm, nk), _rows(tm, nk)],
        out_shape=[jax.ShapeDtypeStruct((b, l, nq), BF16), jax.ShapeDtypeStruct((b, l, nk), BF16),
                   jax.ShapeDtypeStruct((b, l, nk), BF16)],
        compiler_params=_cparams("parallel", "parallel"),
        name="swa_proj_lat" if rope else "swa_proj_ctx",
    )(*ins)


def _softmax_pv(s_list, v_list, sink):
    m = None
    for s in s_list:
        ms = jnp.max(s, axis=-1, keepdims=True)
        m = ms if m is None else jnp.maximum(m, ms)
    if sink is not None:
        m = jnp.maximum(m, sink)
    denom = None
    o = None
    for s, v in zip(s_list, v_list):
        p = jnp.exp2(s - m)
        ps = jnp.sum(p, axis=-1, keepdims=True)
        denom = ps if denom is None else denom + ps
        pv = _dot(p.astype(BF16), v)
        o = pv if o is None else o + pv
    if sink is not None:
        denom = denom + jnp.exp2(sink - m)
    return o / denom


def _swa_attn_body(has_win, tq, seq, sink_ref, q_ref, *refs):
    if has_win:
        kd_ref, vd_ref, kc_ref, vc_ref, o_ref = refs
    else:
        kc_ref, vc_ref, o_ref = refs
    hk = pl.program_id(1)
    kc, vc = kc_ref[0], vc_ref[0]
    if has_win:
        span = tq + 2 * WINDOW
        s0 = pl.program_id(2) * tq
        ws = pl.multiple_of(jnp.clip(s0 - WINDOW, 0, seq - span), LANES)
        kw = kd_ref[0, pl.ds(ws, span), :]
        vw = vd_ref[0, pl.ds(ws, span), :]
        qpos = s0 + lax.broadcasted_iota(jnp.int32, (tq, span), 0)
        kpos = ws + lax.broadcasted_iota(jnp.int32, (tq, span), 1)
        bias = jnp.where(jnp.abs(qpos - kpos) <= WINDOW, 0.0, NEG_INF).astype(F32)
    lane = lax.broadcasted_iota(jnp.int32, (tq, LANES), 1)
    low = lane < HEAD_DIM
    outs = []
    for pr in range(2):
        qp = q_ref[0, :, pr * LANES:(pr + 1) * LANES]
        o2 = []
        for e in range(2):
            qm = jnp.where(low if e == 0 else jnp.logical_not(low), qp, jnp.zeros_like(qp))
            s_list, v_list = [], []
            if has_win:
                s_list.append(_dot_t(qm, kw) + bias)
                v_list.append(vw)
            s_list.append(_dot_t(qm, kc))
            v_list.append(vc)
            sink = sink_ref[hk * 4 + 2 * pr + e]
            o2.append(_softmax_pv(s_list, v_list, sink))
        outs.append(jnp.where(low, o2[0], o2[1]))
    o_ref[0] = jnp.concatenate(outs, axis=1).astype(o_ref.dtype)


def _swa_attn(sink, q, kd, vd, kcd, vcd, tq):
    b, l, dq = q.shape
    n_ctx = kcd.shape[1]
    has_win = kd is not None
    gw = dq // SWA_KV_HEADS
    ins = [sink, q]
    specs = [pl.BlockSpec(memory_space=pltpu.SMEM),
             pl.BlockSpec((1, tq, gw), lambda bb, h, i: (bb, i, h))]
    if has_win:
        ins += [kd, vd]
        specs += [pl.BlockSpec((1, l, LANES), lambda bb, h, i: (bb, 0, h))] * 2
    ins += [kcd, vcd]
    specs += [pl.BlockSpec((1, n_ctx, LANES), lambda bb, h, i: (bb, 0, h))] * 2
    return pl.pallas_call(
        functools.partial(_swa_attn_body, has_win, tq, l),
        grid=(b, SWA_KV_HEADS, l // tq),
        in_specs=specs,
        out_specs=pl.BlockSpec((1, tq, gw), lambda bb, h, i: (bb, i, h)),
        out_shape=jax.ShapeDtypeStruct((b, l, dq), BF16),
        compiler_params=_cparams("parallel", "parallel", "parallel"),
        name="swa_attn_lat" if has_win else "swa_attn_ctx",
    )(*ins)


def _mla_proj_body(rope, x_ref, gain_ref, sh_ref, sc_ref, wd_ref, gql_ref, wuq_ref, gkvl_ref,
                   wuk_ref, wuv_ref, gq_ref, gk_ref, *rest):
    if rope:
        cos_ref, sin_ref, q_out, k_out, v_out = rest
        cos, sin = cos_ref[...], sin_ref[...]
    else:
        q_out, k_out, v_out = rest
    h = _modnorm(x_ref[0], gain_ref[...], sh_ref[0], sc_ref[0]).astype(BF16)
    t = _dot(h, wd_ref[...])
    cq = t[:, :Q_LORA]
    ckv = t[:, Q_LORA:Q_LORA + KV_LORA]
    kr = t[:, Q_LORA + KV_LORA:]
    cq = (cq * lax.rsqrt(jnp.mean(cq * cq, axis=-1, keepdims=True) + EPS) * gql_ref[...]).astype(BF16)
    ckv = (ckv * lax.rsqrt(jnp.mean(ckv * ckv, axis=-1, keepdims=True) + EPS) * gkvl_ref[...]).astype(BF16)
    qa = _dot(cq, wuq_ref[...])
    ka = _dot(ckv, wuk_ref[...])
    v_out[0] = _dot(ckv, wuv_ref[...]).astype(BF16)
    gq, gk = gq_ref[...], gk_ref[...]
    for hh in range(MLA_HEADS):
        qh = qa[:, hh * LANES:(hh + 1) * LANES]
        qh = qh * lax.rsqrt(jnp.sum(qh * qh, axis=-1, keepdims=True) * (1.0 / MLA_QK) + EPS) * gq
        kh = ka[:, hh * LANES:(hh + 1) * LANES] + kr
        kh = kh * lax.rsqrt(jnp.sum(kh * kh, axis=-1, keepdims=True) * (1.0 / MLA_QK) + EPS) * gk
        if rope:
            qh = _rope128(qh, cos, sin, 8)
            kh = _rope128(kh, cos, sin, 8)
        q_out[0, hh] = qh.astype(BF16)
        k_out[0, hh] = kh.astype(BF16)


def _mla_proj(x, gain, sh, sc, w, tables, tm):
    b, l, d = x.shape
    rope = tables is not None
    ins = [x, gain, sh, sc] + list(w)
    specs = [_rows(tm, d), _const(gain.shape), _vec(d), _vec(d)] + [_const(a.shape) for a in w]
    if rope:
        ins += list(tables)
        specs += [pl.BlockSpec((tm, LANES), lambda bb, i: (i, 0))] * 2
    hspec = pl.BlockSpec((1, MLA_HEADS, tm, LANES), lambda bb, i: (bb, 0, i, 0))
    hshape = jax.ShapeDtypeStruct((b, MLA_HEADS, l, LANES), BF16)
    nv = w[5].shape[1]
    return pl.pallas_call(
        functools.partial(_mla_proj_body, rope),
        grid=(b, l // tm),
        in_specs=specs,
        out_specs=[hspec, hspec, _rows(tm, nv)],
        out_shape=[hshape, hshape, jax.ShapeDtypeStruct((b, l, nv), BF16)],
        compiler_params=_cparams("parallel", "parallel"),
        name="mla_proj_lat" if rope else "mla_proj_ctx",
    )(*ins)


def _mla_attn_body(has_lat, q_ref, *refs):
    if has_lat:
        kl_ref, vl_ref, kc_ref, vc_ref, o_ref = refs
    else:
        kc_ref, vc_ref, o_ref = refs
    tq = q_ref.shape[2]
    low = lax.broadcasted_iota(jnp.int32, (tq, LANES), 1) < MLA_NOPE
    o2 = []
    for e in range(2):
        q = q_ref[0, e]
        s_list, v_list = [], []
        if has_lat:
            s_list.append(_dot_t(q, kl_ref[0, e]))
            v_list.append(vl_ref[0])
        s_list.append(_dot_t(q, kc_ref[0, e]))
        v_list.append(vc_ref[0])
        o2.append(_softmax_pv(s_list, v_list, None))
    o_ref[0] = jnp.where(low, o2[0], o2[1]).astype(o_ref.dtype)


def _mla_attn(q, kl, vl, kc, vc, tq):
    b, nh, l, _ = q.shape
    n_ctx = kc.shape[2]
    has_lat = kl is not None
    ins = [q]
    specs = [pl.BlockSpec((1, 2, tq, LANES), lambda bb, j, i: (bb, j, i, 0))]
    if has_lat:
        ins += [kl, vl]
        specs += [pl.BlockSpec((1, 2, l, LANES), lambda bb, j, i: (bb, j, 0, 0)),
                  pl.BlockSpec((1, l, LANES), lambda bb, j, i: (bb, 0, j))]
    ins += [kc, vc]
    specs += [pl.BlockSpec((1, 2, n_ctx, LANES), lambda bb, j, i: (bb, j, 0, 0)),
              pl.BlockSpec((1, n_ctx, LANES), lambda bb, j, i: (bb, 0, j))]
    return pl.pallas_call(
        functools.partial(_mla_attn_body, has_lat),
        grid=(b, nh // 2, l // tq),
        in_specs=specs,
        out_specs=pl.BlockSpec((1, tq, LANES), lambda bb, j, i: (bb, i, j)),
        out_shape=jax.ShapeDtypeStruct((b, l, nh * MLA_NOPE), BF16),
        compiler_params=_cparams("parallel", "parallel", "parallel"),
        name="mla_attn_lat" if has_lat else "mla_attn_ctx",
    )(*ins)


def _lru_in_body(x_ref, gain_ref, sh_ref, sc_ref, w_ref, g_out, x_out):
    h = _modnorm(x_ref[0], gain_ref[...], sh_ref[0], sc_ref[0]).astype(BF16)
    u = _dot(h, w_ref[...])
    width = u.shape[1] // 2
    g_out[0] = jax.nn.gelu(u[:, :width], approximate=True).astype(BF16)
    x_out[0] = u[:, width:]


def _lru_in(x, gain, sh, sc, w_in, tm):
    b, l, d = x.shape
    width = w_in.shape[1] // 2
    return pl.pallas_call(
        _lru_in_body,
        grid=(b, l // tm),
        in_specs=[_rows(tm, d), _const(gain.shape), _vec(d), _vec(d), _const(w_in.shape)],
        out_specs=[_rows(tm, width), _rows(tm, width)],
        out_shape=[jax.ShapeDtypeStruct((b, l, width), BF16), jax.ShapeDtypeStruct((b, l, width), F32)],
        compiler_params=_cparams("parallel", "parallel"),
        name="lru_in",
    )(x, gain, sh, sc, w_in)


def _halo_specs(tm, l, d):
    nb = tm // HALO
    last = l // HALO - 1
    prev = pl.BlockSpec((1, HALO, d), lambda b, i: (b, jnp.maximum(i * nb - 1, 0), 0))
    nxt = pl.BlockSpec((1, HALO, d), lambda b, i: (b, jnp.minimum((i + 1) * nb, last), 0))
    return prev, nxt


def _lru_coef_body(x_ref, xp_ref, xn_ref, cw_ref, gw_ref, gb_ref, lam_ref, af_out, bf_out, ar_out, br_out):
    i = pl.program_id(1)
    tm = x_ref.shape[1]
    x = x_ref[0]
    pm = (i > 0).astype(F32)
    nm = (i < pl.num_programs(1) - 1).astype(F32)
    p2 = xp_ref[0, HALO - 2:HALO - 1, :] * pm
    p1 = xp_ref[0, HALO - 1:HALO, :] * pm
    n1 = xn_ref[0, 0:1, :] * nm
    row = lax.broadcasted_iota(jnp.int32, x.shape, 0)
    xm1 = jnp.where(row == 0, p1, pltpu.roll(x, 1, 0))
    xm2 = jnp.where(row == 0, p2, jnp.where(row == 1, p1, pltpu.roll(x, 2, 0)))
    xp1 = jnp.where(row == tm - 1, n1, pltpu.roll(x, tm - 1, 0))
    cw = cw_ref[...]
    xc = cw[0:1] * xm2 + cw[1:2] * xm1 + cw[2:3] * x + cw[3:4] * xp1 + cw[4:5]
    xcb = xc.astype(BF16)
    outs = ((af_out, bf_out), (ar_out, br_out))
    for d in range(2):
        gates = []
        for k in range(2):
            g = jnp.concatenate(
                [_dot(xcb[:, n * LRU_BW:(n + 1) * LRU_BW], gw_ref[(d * 2 + k) * LRU_BLOCKS + n])
                 for n in range(LRU_BLOCKS)], axis=1)
            gates.append(jax.nn.sigmoid(g + gb_ref[d * 2 + k:d * 2 + k + 1, :]))
        r, ig = gates
        nl = -lam_ref[d:d + 1, :]
        softplus = jnp.maximum(nl, 0.0) + jnp.log1p(jnp.exp(-jnp.abs(nl)))
        log_a = (-LRU_C) * r * softplus
        a = jnp.exp(log_a)
        outs[d][0][0] = a
        outs[d][1][0] = jnp.sqrt(-jnp.tanh(log_a) * (a * a + 1.0)) * (ig * xc)


def _lru_coef(xr, cw, gw, gb, lam, tm):
    b, l, d = xr.shape
    prev, nxt = _halo_specs(tm, l, d)
    out = jax.ShapeDtypeStruct((b, l, d), F32)
    return pl.pallas_call(
        _lru_coef_body,
        grid=(b, l // tm),
        in_specs=[_rows(tm, d), prev, nxt, _const(cw.shape), _const(gw.shape), _const(gb.shape),
                  _const(lam.shape)],
        out_specs=[_rows(tm, d)] * 4,
        out_shape=[out] * 4,
        compiler_params=_cparams("parallel", "parallel"),
        name="lru_coef",
    )(xr, xr, xr, cw, gw, gb, lam)


def _scan_body(reverse, has_add, a_ref, b_ref, h0_ref, *rest):
    if has_add:
        add_ref, o_ref, he_ref, h_s = rest
    else:
        o_ref, he_ref, h_s = rest
    nb, tc = a_ref.shape[0], a_ref.shape[1]

    @pl.when(pl.program_id(0) == 0)
    def _():
        h_s[...] = h0_ref[...]

    def step(j, hc):
        t = tc - 1 - j if reverse else j
        new = []
        for bb in range(nb):
            hn = a_ref[bb, t] * hc[bb] + b_ref[bb, t]
            o_ref[bb, t] = hn + add_ref[bb, t] if has_add else hn
            new.append(hn)
        return tuple(new)

    hc = lax.fori_loop(0, tc, step, tuple(h_s[bb] for bb in range(nb)), unroll=4)
    for bb in range(nb):
        h_s[bb] = hc[bb]
        he_ref[bb] = hc[bb]


def _scan(a, b, h0, add, reverse, tc):
    nb, t, d = a.shape
    sub = d // LANES
    shp = (nb, t, sub, LANES)
    nt = t // tc
    imap = (lambda i: (0, nt - 1 - i, 0, 0)) if reverse else (lambda i: (0, i, 0, 0))
    blk = pl.BlockSpec((nb, tc, sub, LANES), imap)
    hspec = pl.BlockSpec((nb, sub, LANES), lambda i: (0, 0, 0))
    ins = [a.reshape(shp), b.reshape(shp), h0]
    specs = [blk, blk, hspec]
    if add is not None:
        ins.append(add.reshape(shp))
        specs.append(blk)
    out, h_end = pl.pallas_call(
        functools.partial(_scan_body, reverse, add is not None),
        grid=(nt,),
        in_specs=specs,
        out_specs=[blk, hspec],
        out_shape=[jax.ShapeDtypeStruct(shp, F32), jax.ShapeDtypeStruct((nb, sub, LANES), F32)],
        scratch_shapes=[pltpu.VMEM((nb, sub, LANES), F32)],
        compiler_params=_cparams("arbitrary"),
        name="lru_scan_rev" if reverse else "lru_scan_fwd",
    )(*ins)
    return out.reshape(nb, t, d), h_end


def _outproj_body(has_r, o_ref, *rest):
    if has_r:
        r_ref, w_ref, x_ref, g_ref, y_ref = rest
        o = (o_ref[0].astype(F32) * r_ref[0]).astype(BF16)
    else:
        w_ref, x_ref, g_ref, y_ref = rest
        o = o_ref[0]
    y_ref[0] = x_ref[0] + g_ref[0] * _dot(o, w_ref[...])


def _outproj(o, r, w, x, g, tm):
    b, l, d = x.shape
    dk = o.shape[2]
    ins = [o] + ([r] if r is not None else []) + [w, x, g]
    specs = [_rows(tm, dk)] + ([_rows(tm, dk)] if r is not None else []) + [_const(w.shape), _rows(tm, d), _vec(d)]
    return pl.pallas_call(
        functools.partial(_outproj_body, r is not None),
        grid=(b, l // tm),
        in_specs=specs,
        out_specs=_rows(tm, d),
        out_shape=jax.ShapeDtypeStruct((b, l, d), F32),
        compiler_params=_cparams("parallel", "parallel"),
        name="outproj",
    )(*ins)


def _ffn_body(x_ref, xp_ref, xn_ref, gain_ref, sh_ref, sc_ref, g_ref, wg_ref, wv_ref, cw_ref, wd_ref,
              y_ref, h_s, acc_s):
    i = pl.program_id(1)
    tm = x_ref.shape[1]
    gain, sh, sc = gain_ref[...], sh_ref[0], sc_ref[0]
    x = x_ref[0]
    h_s[0:tm, :] = _modnorm(x, gain, sh, sc).astype(BF16)
    halo = jnp.concatenate([xp_ref[0], xn_ref[0]], axis=0)
    h_s[tm:tm + 2 * HALO, :] = _modnorm(halo, gain, sh, sc).astype(BF16)
    pm = (i > 0).astype(F32)
    nm = (i < pl.num_programs(1) - 1).astype(F32)
    acc_s[...] = jnp.zeros_like(acc_s)

    def chunk(c, carry):
        ge = _dot(h_s[...], wg_ref[c])
        val = _dot(h_s[0:tm, :], wv_ref[c])
        g = ge[0:tm]
        gp = ge[tm + HALO - 1:tm + HALO] * pm
        gn = ge[tm + HALO:tm + HALO + 1] * nm
        row = lax.broadcasted_iota(jnp.int32, g.shape, 0)
        gd = jnp.where(row == 0, gp, pltpu.roll(g, 1, 0))
        gu = jnp.where(row == tm - 1, gn, pltpu.roll(g, tm - 1, 0))
        cw = cw_ref[c]
        y = cw[0:1] * gd + cw[1:2] * g + cw[2:3] * gu + cw[3:4]
        act = (y * jax.nn.sigmoid(y) * val).astype(BF16)
        acc_s[...] += _dot(act, wd_ref[c])
        return carry

    lax.fori_loop(0, wg_ref.shape[0], chunk, 0)
    y_ref[0] = x + g_ref[0] * acc_s[...]


def _ffn(x, gain, sh, sc, g, w, tm):
    b, l, d = x.shape
    wg, wv, cw, wd = w
    prev, nxt = _halo_specs(tm, l, d)
    return pl.pallas_call(
        _ffn_body,
        grid=(b, l // tm),
        in_specs=[_rows(tm, d), prev, nxt, _const(gain.shape), _vec(d), _vec(d), _vec(d),
                  _const(wg.shape), _const(wv.shape), _const(cw.shape), _const(wd.shape)],
        out_specs=_rows(tm, d),
        out_shape=jax.ShapeDtypeStruct((b, l, d), F32),
        scratch_shapes=[pltpu.VMEM((tm + 2 * HALO, d), BF16), pltpu.VMEM((tm, d), F32)],
        compiler_params=_cparams("parallel", "parallel"),
        name="ffn",
    )(x, x, x, gain, sh, sc, g, wg, wv, cw, wd)


def _rope_cs(n_tokens, rot_dim):
    pos = jnp.arange(n_tokens)
    row = (pos // GRID_W).astype(F32)
    col = (pos % GRID_W).astype(F32)
    n_freq = rot_dim // 4
    inv = ROPE_BASE ** (-jnp.arange(n_freq, dtype=F32) / n_freq)
    ar, ac = row[:, None] * inv, col[:, None] * inv
    cos = jnp.concatenate([jnp.cos(ar), jnp.cos(ar), jnp.cos(ac), jnp.cos(ac)], axis=1)
    sin = jnp.concatenate([-jnp.sin(ar), jnp.sin(ar), -jnp.sin(ac), jnp.sin(ac)], axis=1)
    return cos, sin


def _pad_rows(a, rows):
    return jnp.pad(a, ((0, rows - a.shape[0]), (0, 0)))


def kernel(x, c, ctx, c_ctx, norm1, norm2, mod_w, mod_b, swa_w_qkv, swa_q_gain, swa_k_gain, swa_sink, swa_w_o, mla_w_down, mla_q_lora_gain, mla_w_uq, mla_kv_lora_gain, mla_w_uk, mla_w_uv, mla_q_gain, mla_k_gain, mla_w_o, lru_w_in, lru_conv_w, lru_conv_b, lru_gate_w, lru_gate_b, lru_lam, lru_w_out, ffn_w_up, ffn_conv_w, ffn_conv_b, ffn_w_down):
    bsz, seq, d = x.shape
    n_ctx = ctx.shape[1]
    depth = norm1.shape[0]
    tm_lat, tm_ctx = 512, n_ctx

    cond = _pad_rows(jnp.concatenate([c, c_ctx[None, :]], axis=0), 16)
    mods = _mod_all(cond, mod_w, mod_b)

    cos_s, sin_s = _rope_cs(seq, HEAD_DIM)
    swa_tables = (jnp.tile(cos_s, (1, 2)), jnp.tile(sin_s, (1, 2)))
    cos_m, sin_m = _rope_cs(seq, MLA_ROPE)
    one = jnp.ones((seq, MLA_NOPE), F32)
    zero = jnp.zeros((seq, MLA_NOPE), F32)
    mla_tables = (jnp.concatenate([one, cos_m, one[:, :LANES - MLA_QK]], axis=1),
                  jnp.concatenate([zero, sin_m, zero[:, :LANES - MLA_QK]], axis=1))
    head_of = jnp.arange(SWA_HEADS * HEAD_DIM) // HEAD_DIM
    block_ones = (head_of[:, None] == head_of[None, :]).astype(BF16)

    for layer in range(depth):
        kind, idx = layer % 3, layer // 3
        with_ctx_out = layer < depth - 1
        m = mods[layer].reshape(16, 6, d)
        ml = [m[:bsz, j][:, None, :] for j in range(6)]
        mc = [jnp.broadcast_to(m[bsz, j][None, None, :], (bsz, 1, d)) for j in range(6)]
        n1 = norm1[layer][None, :]
        n2 = norm2[layer][None, :]
        r_lat = r_ctx = None
        if kind == 0:
            wqkv = swa_w_qkv[idx]
            nq = SWA_HEADS * HEAD_DIM
            nkv = SWA_KV_HEADS * HEAD_DIM

            def dup(w):
                w = w.reshape(d, SWA_KV_HEADS, 1, HEAD_DIM)
                return jnp.broadcast_to(w, (d, SWA_KV_HEADS, 2, HEAD_DIM)).reshape(d, 2 * nkv).astype(BF16)

            w = (wqkv[:, :nq].astype(BF16), dup(wqkv[:, nq:nq + nkv]), dup(wqkv[:, nq + nkv:]), block_ones,
                 (jnp.tile(swa_q_gain[idx], SWA_HEADS) * (HEAD_DIM ** -0.5 * LOG2E))[None, :],
                 jnp.tile(swa_k_gain[idx], 2 * SWA_KV_HEADS)[None, :])
            sink = swa_sink[idx] * LOG2E
            qc, kc, vc = _swa_proj(ctx, n1, mc[0], mc[1], w, None, tm_ctx)
            ql, kl, vl = _swa_proj(x, n1, ml[0], ml[1], w, swa_tables, tm_lat)
            o_lat = _swa_attn(sink, ql, kl, vl, kc, vc, 256)
            o_ctx = _swa_attn(sink, qc, None, None, kc, vc, n_ctx) if with_ctx_out else None
            w_o = swa_w_o[idx].astype(BF16)
        elif kind == 1:
            wdn = mla_w_down[idx]
            zc = jnp.zeros((d, MLA_NOPE), F32)
            wd_pad = jnp.concatenate([wdn[:, :Q_LORA + KV_LORA], zc, wdn[:, Q_LORA + KV_LORA:],
                                      zc[:, :LANES - MLA_QK]], axis=1).astype(BF16)
            wuq = jnp.pad(mla_w_uq[idx].reshape(Q_LORA, MLA_HEADS, MLA_QK),
                          ((0, 0), (0, 0), (0, LANES - MLA_QK))).reshape(Q_LORA, MLA_HEADS * LANES).astype(BF16)
            wuk = jnp.pad(mla_w_uk[idx].reshape(KV_LORA, MLA_HEADS, MLA_NOPE),
                          ((0, 0), (0, 0), (0, LANES - MLA_NOPE))).reshape(KV_LORA, MLA_HEADS * LANES).astype(BF16)
            gq = jnp.pad(mla_q_gain[idx] * (MLA_QK ** -0.5 * LOG2E), (0, LANES - MLA_QK))[None, :]
            gk = jnp.pad(mla_k_gain[idx], (0, LANES - MLA_QK))[None, :]
            w = (wd_pad, mla_q_lora_gain[idx][None, :], wuq, mla_kv_lora_gain[idx][None, :], wuk,
                 mla_w_uv[idx].astype(BF16), gq, gk)
            qc, kc, vc = _mla_proj(ctx, n1, mc[0], mc[1], w, None, tm_ctx)
            ql, kl, vl = _mla_proj(x, n1, ml[0], ml[1], w, mla_tables, tm_lat)
            o_lat = _mla_attn(ql, kl, vl, kc, vc, 256)
            o_ctx = _mla_attn(qc, None, None, kc, vc, n_ctx) if with_ctx_out else None
            w_o = mla_w_o[idx].astype(BF16)
        else:
            w_in = lru_w_in[idx].astype(BF16)
            cw = _pad_rows(jnp.concatenate([lru_conv_w[idx], lru_conv_b[idx][None, :]], axis=0), 8)
            gw = lru_gate_w[idx].reshape(2 * 2 * LRU_BLOCKS, LRU_BW, LRU_BW).astype(BF16)
            gb = _pad_rows(lru_gate_b[idx].reshape(4, d), 8)
            lam = _pad_rows(lru_lam[idx], 8)
            g_ctx, xr_ctx = _lru_in(ctx, n1, mc[0], mc[1], w_in, tm_ctx)
            g_lat, xr_lat = _lru_in(x, n1, ml[0], ml[1], w_in, tm_lat)
            afc, bfc, arc, brc = _lru_coef(xr_ctx, cw, gw, gb, lam, tm_ctx)
            afl, bfl, arl, brl = _lru_coef(xr_lat, cw, gw, gb, lam, tm_lat)
            h0 = jnp.zeros((bsz, d // LANES, LANES), F32)
            tc = 64
            sfc, hfc = _scan(afc, bfc, h0, None, False, tc)
            r_ctx, hrc = _scan(arc, brc, h0, sfc, True, tc)
            sfl, _ = _scan(afl, bfl, hfc, None, False, tc)
            r_lat, _ = _scan(arl, brl, hrc, sfl, True, tc)
            o_lat, o_ctx = g_lat, (g_ctx if with_ctx_out else None)
            w_o = lru_w_out[idx].astype(BF16)

        w_up = ffn_w_up[layer]
        n_chunk = D_FF // FF_CHUNK

        def chunks(wcols):
            return jnp.transpose(wcols.reshape(d, n_chunk, FF_CHUNK), (1, 0, 2)).astype(BF16)

        cw_ffn = jnp.concatenate([ffn_conv_w[layer], ffn_conv_b[layer][None, :]], axis=0)
        cw_ffn = jnp.transpose(cw_ffn.reshape(4, n_chunk, FF_CHUNK), (1, 0, 2))
        cw_ffn = jnp.pad(cw_ffn, ((0, 0), (0, 4), (0, 0)))
        wf = (chunks(w_up[:, :D_FF]), chunks(w_up[:, D_FF:]), cw_ffn,
              ffn_w_down[layer].reshape(n_chunk, FF_CHUNK, d).astype(BF16))

        x = _outproj(o_lat, r_lat, w_o, x, ml[2], tm_lat)
        x = _ffn(x, n2, ml[3], ml[4], ml[5], wf, tm_lat)
        if with_ctx_out:
            ctx = _outproj(o_ctx, r_ctx, w_o, ctx, mc[2], tm_ctx)
            ctx = _ffn(ctx, n2, mc[3], mc[4], mc[5], wf, tm_ctx)
    return x
```

```python
import functools

import jax
import jax.numpy as jnp
from jax import lax
from jax.experimental import pallas as pl
from jax.experimental.pallas import tpu as pltpu

F32 = jnp.float32
BF16 = jnp.bfloat16

GRID_W = 64
HEAD_DIM = 64
SWA_HEADS = 16
SWA_KV_HEADS = 4
WINDOW = 128
MLA_HEADS = 16
MLA_NOPE = 64
MLA_ROPE = 32
MLA_QK = MLA_NOPE + MLA_ROPE
Q_LORA = 384
KV_LORA = 256
LRU_BLOCKS = 4
LRU_BW = 256
LRU_C = 8.0
D_FF = 2816
ROPE_BASE = 10000.0
EPS = 1e-6
NEG_INF = -1e30
LOG2E = 1.4426950408889634

LANES = 128
FF_CHUNK = 256
HALO = 8
MLA_KEY_BLOCK = 512
VMEM_LIMIT = 56 * 1024 * 1024


def _cparams(*sem):
    return pltpu.CompilerParams(dimension_semantics=sem, vmem_limit_bytes=VMEM_LIMIT)


def _const(shape):
    nd = len(shape)
    return pl.BlockSpec(shape, lambda *_: (0,) * nd, pipeline_mode=pl.Buffered(1))


def _rows(tm, d):
    return pl.BlockSpec((1, tm, d), lambda b, i: (b, i, 0))


def _vec(d):
    return pl.BlockSpec((1, 1, d), lambda b, i: (b, 0, 0))


def _dot(a, b):
    return jnp.dot(a, b, preferred_element_type=F32)


def _dot_t(a, b):
    return lax.dot_general(a, b, (((1,), (1,)), ((), ())), preferred_element_type=F32)


def _modnorm(x, gain, shift, scale):
    ms = jnp.mean(x * x, axis=-1, keepdims=True)
    y = x * lax.rsqrt(ms + EPS) * gain
    return y * (1.0 + scale) + shift


def _rope128(x, cos, sin, half):
    lane = lax.broadcasted_iota(jnp.int32, x.shape, 1)
    up = pltpu.roll(x, LANES - half, 1)
    dn = pltpu.roll(x, half, 1)
    return x * cos + jnp.where((lane & half) == 0, up, dn) * sin


def _mod_body(c_ref, w_ref, b_ref, o_ref):
    c = c_ref[...]
    cs = (c * jax.nn.sigmoid(c)).astype(BF16)
    o_ref[0] = _dot(cs, w_ref[0].astype(BF16)) + b_ref[0]


def _mod_all(cond, mod_w, mod_b):
    depth, d, n = mod_w.shape
    tn = 1536
    rows = cond.shape[0]
    return pl.pallas_call(
        _mod_body,
        grid=(depth, n // tn),
        in_specs=[pl.BlockSpec((rows, d), lambda l, j: (0, 0)),
                  pl.BlockSpec((1, d, tn), lambda l, j: (l, 0, j)),
                  pl.BlockSpec((1, 1, tn), lambda l, j: (l, 0, j))],
        out_specs=pl.BlockSpec((1, rows, tn), lambda l, j: (l, 0, j)),
        out_shape=jax.ShapeDtypeStruct((depth, rows, n), F32),
        compiler_params=_cparams("arbitrary", "arbitrary"),
        name="mod",
    )(cond, mod_w, mod_b.reshape(depth, 1, n))


def _swa_proj_body(rope, x_ref, gain_ref, sh_ref, sc_ref, wq_ref, wk_ref, wv_ref, bd_ref,
                   gq_ref, gk_ref, *rest):
    if rope:
        cos_ref, sin_ref, q_out, k_out, v_out = rest
    else:
        q_out, k_out, v_out = rest
    h = _modnorm(x_ref[0], gain_ref[...], sh_ref[0], sc_ref[0]).astype(BF16)
    q = _dot(h, wq_ref[...])
    k = _dot(h, wk_ref[...])
    v_out[0] = _dot(h, wv_ref[...]).astype(BF16)
    ssq = _dot((q * q).astype(BF16), bd_ref[...])
    q = q * lax.rsqrt(ssq * (1.0 / HEAD_DIM) + EPS) * gq_ref[...]
    ks = []
    for g in range(SWA_KV_HEADS):
        kg = k[:, g * LANES:(g + 1) * LANES]
        ks.append(kg * lax.rsqrt(jnp.mean(kg * kg, axis=-1, keepdims=True) + EPS))
    k = jnp.concatenate(ks, axis=1) * gk_ref[...]
    if rope:
        cos, sin = cos_ref[...], sin_ref[...]
        q = jnp.concatenate([_rope128(q[:, j * LANES:(j + 1) * LANES], cos, sin, 16)
                             for j in range(q.shape[1] // LANES)], axis=1)
        k = jnp.concatenate([_rope128(k[:, j * LANES:(j + 1) * LANES], cos, sin, 16)
                             for j in range(k.shape[1] // LANES)], axis=1)
    q_out[0] = q.astype(BF16)
    k_out[0] = k.astype(BF16)


def _swa_proj(x, gain, sh, sc, w, tables, tm):
    b, l, d = x.shape
    wq, wk, wv, bd, gq, gk = w
    rope = tables is not None
    ins = [x, gain, sh, sc, wq, wk, wv, bd, gq, gk]
    specs = [_rows(tm, d), _const(gain.shape), _vec(d), _vec(d), _const(wq.shape), _const(wk.shape),
             _const(wv.shape), _const(bd.shape), _const(gq.shape), _const(gk.shape)]
    if rope:
        ins += list(tables)
        specs += [pl.BlockSpec((tm, LANES), lambda bb, i: (i, 0))] * 2
    nq, nk = wq.shape[1], wk.shape[1]
    return pl.pallas_call(
        functools.partial(_swa_proj_body, rope),
        grid=(b, l // tm),
        in_specs=specs,
        out_specs=[_rows(tm, nq), _rows(tm, nk), _rows(tm, nk)],
        out_shape=[jax.ShapeDtypeStruct((b, l, nq), BF16), jax.ShapeDtypeStruct((b, l, nk), BF16),
                   jax.ShapeDtypeStruct((b, l, nk), BF16)],
        compiler_params=_cparams("parallel", "parallel"),
        name="swa_proj_lat" if rope else "swa_proj_ctx",
    )(*ins)


def _fold(x, op):
    acc = x[:, :LANES]
    for t in range(1, x.shape[1] // LANES):
        acc = op(acc, x[:, t * LANES:(t + 1) * LANES])
    return acc


def _attend(chains, s_scr):
    def phase1(ci):
        q, parts, _ = chains[ci]
        st = {"mx": None}

        def make(p, off):
            def step():
                k_thunk, _, bias, nk = parts[p]
                s = _dot_t(q, k_thunk())
                if bias is not None:
                    s = s + bias
                s_scr[ci % 2, :, off:off + nk] = s
                f = _fold(s, jnp.maximum)
                st["mx"] = f if st["mx"] is None else jnp.maximum(st["mx"], f)
            return step

        steps, off = [], 0
        for p, part in enumerate(parts):
            steps.append(make(p, off))
            off += part[3]
        return st, steps

    def phase2(ci, st):
        _, parts, sink = chains[ci]
        out = {"sum": None, "acc": [None, None]}

        def make(p, off):
            def step():
                if p == 0:
                    m = jnp.max(st["mx"], axis=-1, keepdims=True)
                    out["m"] = m if sink is None else jnp.maximum(m, sink)
                nk = parts[p][3]
                pr = jnp.exp2(s_scr[ci % 2, :, off:off + nk] - out["m"])
                f = _fold(pr, jnp.add)
                out["sum"] = f if out["sum"] is None else out["sum"] + f
                pv = _dot(pr.astype(BF16), parts[p][1]())
                out["acc"][p % 2] = pv if out["acc"][p % 2] is None else out["acc"][p % 2] + pv
            return step

        def finish():
            den = jnp.sum(out["sum"], axis=-1, keepdims=True)
            if sink is not None:
                den = den + jnp.exp2(sink - out["m"])
            acc = out["acc"][0] if out["acc"][1] is None else out["acc"][0] + out["acc"][1]
            return acc / den

        steps, off = [], 0
        for p, part in enumerate(parts):
            steps.append(make(p, off))
            off += part[3]
        return steps, finish

    results = []
    pending = None
    for ci in range(len(chains) + 1):
        st, steps1 = phase1(ci) if ci < len(chains) else (None, [])
        steps2 = pending[0] if pending else []
        for k in range(max(len(steps1), len(steps2))):
            if k < len(steps1):
                steps1[k]()
            if k < len(steps2):
                steps2[k]()
        if pending:
            results.append(pending[1]())
        pending = phase2(ci, st) if ci < len(chains) else None
    return results


def _swa_attn_body(has_win, tq, sub, seq, sink_ref, q_ref, *refs):
    if has_win:
        kd_ref, vd_ref, kc_ref, vc_ref, o_ref, s_scr = refs
    else:
        kc_ref, vc_ref, o_ref, s_scr = refs
    hk = pl.program_id(1)
    n_ctx = kc_ref.shape[1]
    span = sub + 2 * WINDOW
    low = lax.broadcasted_iota(jnp.int32, (sub, LANES), 1) < HEAD_DIM
    chains = []
    for r in range(tq // sub):
        if has_win:
            s0 = pl.program_id(2) * tq + r * sub
            ws = pl.multiple_of(jnp.clip(s0 - WINDOW, 0, seq - span), LANES)
            qpos = s0 + lax.broadcasted_iota(jnp.int32, (sub, span), 0)
            kpos = ws + lax.broadcasted_iota(jnp.int32, (sub, span), 1)
            bias = jnp.where(jnp.abs(qpos - kpos) <= WINDOW, 0.0, NEG_INF).astype(F32)
        for g in range(4):
            qp = q_ref[0, r * sub:(r + 1) * sub, (g // 2) * LANES:(g // 2 + 1) * LANES]
            qm = jnp.where(low if g % 2 == 0 else jnp.logical_not(low), qp, jnp.zeros_like(qp))
            parts = []
            if has_win:
                parts.append((lambda ws=ws: kd_ref[0, pl.ds(ws, span), :],
                              lambda ws=ws: vd_ref[0, pl.ds(ws, span), :], bias, span))
            parts.append((lambda: kc_ref[0], lambda: vc_ref[0], None, n_ctx))
            chains.append((qm, parts, sink_ref[hk * 4 + g]))
    o = _attend(chains, s_scr)
    for r in range(tq // sub):
        o_ref[0, r * sub:(r + 1) * sub, :] = jnp.concatenate(
            [jnp.where(low, o[4 * r], o[4 * r + 1]), jnp.where(low, o[4 * r + 2], o[4 * r + 3])],
            axis=1).astype(o_ref.dtype)


def _swa_attn(sink, q, kd, vd, kcd, vcd, tq, sub):
    b, l, dq = q.shape
    n_ctx = kcd.shape[1]
    has_win = kd is not None
    gw = dq // SWA_KV_HEADS
    ins = [sink, q]
    specs = [pl.BlockSpec(memory_space=pltpu.SMEM),
             pl.BlockSpec((1, tq, gw), lambda bb, h, i: (bb, i, h))]
    if has_win:
        ins += [kd, vd]
        specs += [pl.BlockSpec((1, l, LANES), lambda bb, h, i: (bb, 0, h))] * 2
    ins += [kcd, vcd]
    specs += [pl.BlockSpec((1, n_ctx, LANES), lambda bb, h, i: (bb, 0, h))] * 2
    return pl.pallas_call(
        functools.partial(_swa_attn_body, has_win, tq, sub, l),
        grid=(b, SWA_KV_HEADS, l // tq),
        in_specs=specs,
        out_specs=pl.BlockSpec((1, tq, gw), lambda bb, h, i: (bb, i, h)),
        out_shape=jax.ShapeDtypeStruct((b, l, dq), BF16),
        scratch_shapes=[pltpu.VMEM((2, sub, n_ctx + (sub + 2 * WINDOW if has_win else 0)), F32)],
        compiler_params=_cparams("parallel", "parallel", "parallel"),
        name="swa_attn_lat" if has_win else "swa_attn_ctx",
    )(*ins)


def _mla_proj_body(rope, x_ref, gain_ref, sh_ref, sc_ref, wd_ref, gql_ref, wuq_ref, gkvl_ref,
                   wuk_ref, wuv_ref, gq_ref, gk_ref, *rest):
    if rope:
        cos_ref, sin_ref, q_out, k_out, v_out = rest
        cos, sin = cos_ref[...], sin_ref[...]
    else:
        q_out, k_out, v_out = rest
    h = _modnorm(x_ref[0], gain_ref[...], sh_ref[0], sc_ref[0]).astype(BF16)
    t = _dot(h, wd_ref[...])
    cq = t[:, :Q_LORA]
    ckv = t[:, Q_LORA:Q_LORA + KV_LORA]
    kr = t[:, Q_LORA + KV_LORA:]
    cq = (cq * lax.rsqrt(jnp.mean(cq * cq, axis=-1, keepdims=True) + EPS) * gql_ref[...]).astype(BF16)
    ckv = (ckv * lax.rsqrt(jnp.mean(ckv * ckv, axis=-1, keepdims=True) + EPS) * gkvl_ref[...]).astype(BF16)
    qa = _dot(cq, wuq_ref[...])
    ka = _dot(ckv, wuk_ref[...])
    v_out[0] = _dot(ckv, wuv_ref[...]).astype(BF16)
    gq, gk = gq_ref[...], gk_ref[...]
    for hh in range(MLA_HEADS):
        qh = qa[:, hh * LANES:(hh + 1) * LANES]
        qh = qh * lax.rsqrt(jnp.sum(qh * qh, axis=-1, keepdims=True) * (1.0 / MLA_QK) + EPS) * gq
        kh = ka[:, hh * LANES:(hh + 1) * LANES] + kr
        kh = kh * lax.rsqrt(jnp.sum(kh * kh, axis=-1, keepdims=True) * (1.0 / MLA_QK) + EPS) * gk
        if rope:
            qh = _rope128(qh, cos, sin, 8)
            kh = _rope128(kh, cos, sin, 8)
        q_out[0, hh] = qh.astype(BF16)
        k_out[0, hh] = kh.astype(BF16)


def _mla_proj(x, gain, sh, sc, w, tables, tm):
    b, l, d = x.shape
    rope = tables is not None
    ins = [x, gain, sh, sc] + list(w)
    specs = [_rows(tm, d), _const(gain.shape), _vec(d), _vec(d)] + [_const(a.shape) for a in w]
    if rope:
        ins += list(tables)
        specs += [pl.BlockSpec((tm, LANES), lambda bb, i: (i, 0))] * 2
    hspec = pl.BlockSpec((1, MLA_HEADS, tm, LANES), lambda bb, i: (bb, 0, i, 0))
    hshape = jax.ShapeDtypeStruct((b, MLA_HEADS, l, LANES), BF16)
    nv = w[5].shape[1]
    return pl.pallas_call(
        functools.partial(_mla_proj_body, rope),
        grid=(b, l // tm),
        in_specs=specs,
        out_specs=[hspec, hspec, _rows(tm, nv)],
        out_shape=[hshape, hshape, jax.ShapeDtypeStruct((b, l, nv), BF16)],
        compiler_params=_cparams("parallel", "parallel"),
        name="mla_proj_lat" if rope else "mla_proj_ctx",
    )(*ins)


def _mla_attn_body(has_lat, tk, sub, q_ref, *refs):
    if has_lat:
        kl_ref, vl_ref, kc_ref, vc_ref, o_ref, s_scr = refs
    else:
        kc_ref, vc_ref, o_ref, s_scr = refs
    tq = q_ref.shape[2]
    n_ctx = kc_ref.shape[2]
    low = lax.broadcasted_iota(jnp.int32, (sub, LANES), 1) < MLA_NOPE
    chains = []
    for r in range(tq // sub):
        for e in range(2):
            parts = []
            if has_lat:
                for j in range(kl_ref.shape[2] // tk):
                    parts.append((lambda e=e, j=j: kl_ref[0, e, j * tk:(j + 1) * tk, :],
                                  lambda j=j: vl_ref[0, j * tk:(j + 1) * tk, :], None, tk))
            parts.append((lambda e=e: kc_ref[0, e], lambda: vc_ref[0], None, n_ctx))
            chains.append((q_ref[0, e, r * sub:(r + 1) * sub, :], parts, None))
    o = _attend(chains, s_scr)
    for r in range(tq // sub):
        o_ref[0, r * sub:(r + 1) * sub, :] = jnp.where(low, o[2 * r], o[2 * r + 1]).astype(o_ref.dtype)


def _mla_attn(q, kl, vl, kc, vc, tq, sub):
    b, nh, l, _ = q.shape
    n_ctx = kc.shape[2]
    has_lat = kl is not None
    ins = [q]
    specs = [pl.BlockSpec((1, 2, tq, LANES), lambda bb, j, i: (bb, j, i, 0))]
    if has_lat:
        ins += [kl, vl]
        specs += [pl.BlockSpec((1, 2, l, LANES), lambda bb, j, i: (bb, j, 0, 0)),
                  pl.BlockSpec((1, l, LANES), lambda bb, j, i: (bb, 0, j))]
    ins += [kc, vc]
    specs += [pl.BlockSpec((1, 2, n_ctx, LANES), lambda bb, j, i: (bb, j, 0, 0)),
              pl.BlockSpec((1, n_ctx, LANES), lambda bb, j, i: (bb, 0, j))]
    return pl.pallas_call(
        functools.partial(_mla_attn_body, has_lat, MLA_KEY_BLOCK, sub),
        grid=(b, nh // 2, l // tq),
        in_specs=specs,
        out_specs=pl.BlockSpec((1, tq, LANES), lambda bb, j, i: (bb, i, j)),
        out_shape=jax.ShapeDtypeStruct((b, l, nh * MLA_NOPE), BF16),
        scratch_shapes=[pltpu.VMEM((2, sub, n_ctx + (l if has_lat else 0)), F32)],
        compiler_params=_cparams("parallel", "parallel", "parallel"),
        name="mla_attn_lat" if has_lat else "mla_attn_ctx",
    )(*ins)


def _lru_in_body(x_ref, gain_ref, sh_ref, sc_ref, w_ref, g_out, x_out):
    h = _modnorm(x_ref[0], gain_ref[...], sh_ref[0], sc_ref[0]).astype(BF16)
    u = _dot(h, w_ref[...])
    width = u.shape[1] // 2
    g_out[0] = jax.nn.gelu(u[:, :width], approximate=True).astype(BF16)
    x_out[0] = u[:, width:]


def _lru_in(x, gain, sh, sc, w_in, tm):
    b, l, d = x.shape
    width = w_in.shape[1] // 2
    return pl.pallas_call(
        _lru_in_body,
        grid=(b, l // tm),
        in_specs=[_rows(tm, d), _const(gain.shape), _vec(d), _vec(d), _const(w_in.shape)],
        out_specs=[_rows(tm, width), _rows(tm, width)],
        out_shape=[jax.ShapeDtypeStruct((b, l, width), BF16), jax.ShapeDtypeStruct((b, l, width), F32)],
        compiler_params=_cparams("parallel", "parallel"),
        name="lru_in",
    )(x, gain, sh, sc, w_in)


def _halo_specs(tm, l, d):
    nb = tm // HALO
    last = l // HALO - 1
    prev = pl.BlockSpec((1, HALO, d), lambda b, i: (b, jnp.maximum(i * nb - 1, 0), 0))
    nxt = pl.BlockSpec((1, HALO, d), lambda b, i: (b, jnp.minimum((i + 1) * nb, last), 0))
    return prev, nxt


def _lru_coef_body(x_ref, xp_ref, xn_ref, cw_ref, gw_ref, gb_ref, lam_ref, af_out, bf_out, ar_out, br_out):
    i = pl.program_id(1)
    tm = x_ref.shape[1]
    x = x_ref[0]
    pm = (i > 0).astype(F32)
    nm = (i < pl.num_programs(1) - 1).astype(F32)
    p2 = xp_ref[0, HALO - 2:HALO - 1, :] * pm
    p1 = xp_ref[0, HALO - 1:HALO, :] * pm
    n1 = xn_ref[0, 0:1, :] * nm
    row = lax.broadcasted_iota(jnp.int32, x.shape, 0)
    xm1 = jnp.where(row == 0, p1, pltpu.roll(x, 1, 0))
    xm2 = jnp.where(row == 0, p2, jnp.where(row == 1, p1, pltpu.roll(x, 2, 0)))
    xp1 = jnp.where(row == tm - 1, n1, pltpu.roll(x, tm - 1, 0))
    cw = cw_ref[...]
    xc = cw[0:1] * xm2 + cw[1:2] * xm1 + cw[2:3] * x + cw[3:4] * xp1 + cw[4:5]
    xcb = xc.astype(BF16)
    outs = ((af_out, bf_out), (ar_out, br_out))
    for d in range(2):
        gates = []
        for k in range(2):
            g = jnp.concatenate(
                [_dot(xcb[:, n * LRU_BW:(n + 1) * LRU_BW], gw_ref[(d * 2 + k) * LRU_BLOCKS + n])
                 for n in range(LRU_BLOCKS)], axis=1)
            gates.append(jax.nn.sigmoid(g + gb_ref[d * 2 + k:d * 2 + k + 1, :]))
        r, ig = gates
        nl = -lam_ref[d:d + 1, :]
        softplus = jnp.maximum(nl, 0.0) + jnp.log1p(jnp.exp(-jnp.abs(nl)))
        log_a = (-LRU_C) * r * softplus
        a = jnp.exp(log_a)
        outs[d][0][0] = a
        outs[d][1][0] = jnp.sqrt(-jnp.tanh(log_a) * (a * a + 1.0)) * (ig * xc)


def _lru_coef(xr, cw, gw, gb, lam, tm):
    b, l, d = xr.shape
    prev, nxt = _halo_specs(tm, l, d)
    out = jax.ShapeDtypeStruct((b, l, d), F32)
    return pl.pallas_call(
        _lru_coef_body,
        grid=(b, l // tm),
        in_specs=[_rows(tm, d), prev, nxt, _const(cw.shape), _const(gw.shape), _const(gb.shape),
                  _const(lam.shape)],
        out_specs=[_rows(tm, d)] * 4,
        out_shape=[out] * 4,
        compiler_params=_cparams("parallel", "parallel"),
        name="lru_coef",
    )(xr, xr, xr, cw, gw, gb, lam)


def _scan_body(reverse, has_add, a_ref, b_ref, h0_ref, *rest):
    if has_add:
        add_ref, o_ref, he_ref, h_s = rest
    else:
        o_ref, he_ref, h_s = rest
    nb, tc = a_ref.shape[0], a_ref.shape[1]

    @pl.when(pl.program_id(0) == 0)
    def _():
        h_s[...] = h0_ref[...]

    def step(j, hc):
        t = tc - 1 - j if reverse else j
        new = []
        for bb in range(nb):
            hn = a_ref[bb, t] * hc[bb] + b_ref[bb, t]
            o_ref[bb, t] = hn + add_ref[bb, t] if has_add else hn
            new.append(hn)
        return tuple(new)

    hc = lax.fori_loop(0, tc, step, tuple(h_s[bb] for bb in range(nb)), unroll=4)
    for bb in range(nb):
        h_s[bb] = hc[bb]
        he_ref[bb] = hc[bb]


def _scan(a, b, h0, add, reverse, tc):
    nb, t, d = a.shape
    sub = d // LANES
    shp = (nb, t, sub, LANES)
    nt = t // tc
    imap = (lambda i: (0, nt - 1 - i, 0, 0)) if reverse else (lambda i: (0, i, 0, 0))
    blk = pl.BlockSpec((nb, tc, sub, LANES), imap)
    hspec = pl.BlockSpec((nb, sub, LANES), lambda i: (0, 0, 0))
    ins = [a.reshape(shp), b.reshape(shp), h0]
    specs = [blk, blk, hspec]
    if add is not None:
        ins.append(add.reshape(shp))
        specs.append(blk)
    out, h_end = pl.pallas_call(
        functools.partial(_scan_body, reverse, add is not None),
        grid=(nt,),
        in_specs=specs,
        out_specs=[blk, hspec],
        out_shape=[jax.ShapeDtypeStruct(shp, F32), jax.ShapeDtypeStruct((nb, sub, LANES), F32)],
        scratch_shapes=[pltpu.VMEM((nb, sub, LANES), F32)],
        compiler_params=_cparams("arbitrary"),
        name="lru_scan_rev" if reverse else "lru_scan_fwd",
    )(*ins)
    return out.reshape(nb, t, d), h_end


def _outproj_body(has_r, o_ref, *rest):
    if has_r:
        r_ref, w_ref, x_ref, g_ref, y_ref = rest
        o = (o_ref[0].astype(F32) * r_ref[0]).astype(BF16)
    else:
        w_ref, x_ref, g_ref, y_ref = rest
        o = o_ref[0]
    y_ref[0] = x_ref[0] + g_ref[0] * _dot(o, w_ref[...])


def _outproj(o, r, w, x, g, tm):
    b, l, d = x.shape
    dk = o.shape[2]
    ins = [o] + ([r] if r is not None else []) + [w, x, g]
    specs = [_rows(tm, dk)] + ([_rows(tm, dk)] if r is not None else []) + [_const(w.shape), _rows(tm, d), _vec(d)]
    return pl.pallas_call(
        functools.partial(_outproj_body, r is not None),
        grid=(b, l // tm),
        in_specs=specs,
        out_specs=_rows(tm, d),
        out_shape=jax.ShapeDtypeStruct((b, l, d), F32),
        compiler_params=_cparams("parallel", "parallel"),
        name="outproj",
    )(*ins)


def _ffn_body(x_ref, xp_ref, xn_ref, gain_ref, sh_ref, sc_ref, g_ref, wu_ref, cw_ref, wd_ref,
              y_ref, h_s, act_s):
    i = pl.program_id(1)
    tm = x_ref.shape[1]
    d_ff = wd_ref.shape[0]
    gain, sh, sc = gain_ref[...], sh_ref[0], sc_ref[0]
    h_s[0:tm, :] = _modnorm(x_ref[0], gain, sh, sc).astype(BF16)
    halo = jnp.concatenate([xp_ref[0], xn_ref[0]], axis=0)
    h_s[tm:tm + 2 * HALO, :] = _modnorm(halo, gain, sh, sc).astype(BF16)
    pm = (i > 0).astype(F32)
    nm = (i < pl.num_programs(1) - 1).astype(F32)

    def up(c):
        lo = c * FF_CHUNK
        return (_dot(h_s[...], wu_ref[:, lo:lo + FF_CHUNK]),
                _dot(h_s[0:tm, :], wu_ref[:, d_ff + lo:d_ff + lo + FF_CHUNK]))

    nxt = up(0)
    for c in range(d_ff // FF_CHUNK):
        ge, val = nxt
        if (c + 1) * FF_CHUNK < d_ff:
            nxt = up(c + 1)
        lo = c * FF_CHUNK
        g = ge[0:tm]
        gp = ge[tm + HALO - 1:tm + HALO] * pm
        gn = ge[tm + HALO:tm + HALO + 1] * nm
        row = lax.broadcasted_iota(jnp.int32, g.shape, 0)
        gd = jnp.where(row == 0, gp, pltpu.roll(g, 1, 0))
        gu = jnp.where(row == tm - 1, gn, pltpu.roll(g, tm - 1, 0))
        cw = cw_ref[:, lo:lo + FF_CHUNK]
        y = cw[0:1] * gd + cw[1:2] * g + cw[2:3] * gu + cw[3:4]
        act_s[:, lo:lo + FF_CHUNK] = (y * jax.nn.sigmoid(y) * val).astype(BF16)
    y_ref[0] = x_ref[0] + g_ref[0] * _dot(act_s[...], wd_ref[...])


def _ffn(x, gain, sh, sc, g, w, tm):
    b, l, d = x.shape
    wu, cw, wd = w
    prev, nxt = _halo_specs(tm, l, d)
    return pl.pallas_call(
        _ffn_body,
        grid=(b, l // tm),
        in_specs=[_rows(tm, d), prev, nxt, _const(gain.shape), _vec(d), _vec(d), _vec(d),
                  _const(wu.shape), _const(cw.shape), _const(wd.shape)],
        out_specs=_rows(tm, d),
        out_shape=jax.ShapeDtypeStruct((b, l, d), F32),
        scratch_shapes=[pltpu.VMEM((tm + 2 * HALO, d), BF16), pltpu.VMEM((tm, wd.shape[0]), BF16)],
        compiler_params=_cparams("parallel", "parallel"),
        name="ffn",
    )(x, x, x, gain, sh, sc, g, wu, cw, wd)


def _rope_cs(n_tokens, rot_dim):
    pos = jnp.arange(n_tokens)
    row = (pos // GRID_W).astype(F32)
    col = (pos % GRID_W).astype(F32)
    n_freq = rot_dim // 4
    inv = ROPE_BASE ** (-jnp.arange(n_freq, dtype=F32) / n_freq)
    ar, ac = row[:, None] * inv, col[:, None] * inv
    cos = jnp.concatenate([jnp.cos(ar), jnp.cos(ar), jnp.cos(ac), jnp.cos(ac)], axis=1)
    sin = jnp.concatenate([-jnp.sin(ar), jnp.sin(ar), -jnp.sin(ac), jnp.sin(ac)], axis=1)
    return cos, sin


def _pad_rows(a, rows):
    return jnp.pad(a, ((0, rows - a.shape[0]), (0, 0)))


def kernel(x, c, ctx, c_ctx, norm1, norm2, mod_w, mod_b, swa_w_qkv, swa_q_gain, swa_k_gain, swa_sink, swa_w_o, mla_w_down, mla_q_lora_gain, mla_w_uq, mla_kv_lora_gain, mla_w_uk, mla_w_uv, mla_q_gain, mla_k_gain, mla_w_o, lru_w_in, lru_conv_w, lru_conv_b, lru_gate_w, lru_gate_b, lru_lam, lru_w_out, ffn_w_up, ffn_conv_w, ffn_conv_b, ffn_w_down):
    bsz, seq, d = x.shape
    n_ctx = ctx.shape[1]
    depth = norm1.shape[0]
    tm_lat, tm_ctx = 512, n_ctx

    cond = _pad_rows(jnp.concatenate([c, c_ctx[None, :]], axis=0), 16)
    mods = _mod_all(cond, mod_w, mod_b)

    cos_s, sin_s = _rope_cs(seq, HEAD_DIM)
    swa_tables = (jnp.tile(cos_s, (1, 2)), jnp.tile(sin_s, (1, 2)))
    cos_m, sin_m = _rope_cs(seq, MLA_ROPE)
    one = jnp.ones((seq, MLA_NOPE), F32)
    zero = jnp.zeros((seq, MLA_NOPE), F32)
    mla_tables = (jnp.concatenate([one, cos_m, one[:, :LANES - MLA_QK]], axis=1),
                  jnp.concatenate([zero, sin_m, zero[:, :LANES - MLA_QK]], axis=1))
    head_of = jnp.arange(SWA_HEADS * HEAD_DIM) // HEAD_DIM
    block_ones = (head_of[:, None] == head_of[None, :]).astype(BF16)

    for layer in range(depth):
        kind, idx = layer % 3, layer // 3
        with_ctx_out = layer < depth - 1
        m = mods[layer].reshape(16, 6, d)
        ml = [m[:bsz, j][:, None, :] for j in range(6)]
        mc = [jnp.broadcast_to(m[bsz, j][None, None, :], (bsz, 1, d)) for j in range(6)]
        n1 = norm1[layer][None, :]
        n2 = norm2[layer][None, :]
        r_lat = r_ctx = None
        if kind == 0:
            wqkv = swa_w_qkv[idx]
            nq = SWA_HEADS * HEAD_DIM
            nkv = SWA_KV_HEADS * HEAD_DIM

            def dup(w):
                w = w.reshape(d, SWA_KV_HEADS, 1, HEAD_DIM)
                return jnp.broadcast_to(w, (d, SWA_KV_HEADS, 2, HEAD_DIM)).reshape(d, 2 * nkv).astype(BF16)

            w = (wqkv[:, :nq].astype(BF16), dup(wqkv[:, nq:nq + nkv]), dup(wqkv[:, nq + nkv:]), block_ones,
                 (jnp.tile(swa_q_gain[idx], SWA_HEADS) * (HEAD_DIM ** -0.5 * LOG2E))[None, :],
                 jnp.tile(swa_k_gain[idx], 2 * SWA_KV_HEADS)[None, :])
            sink = swa_sink[idx] * LOG2E
            qc, kc, vc = _swa_proj(ctx, n1, mc[0], mc[1], w, None, tm_ctx)
            ql, kl, vl = _swa_proj(x, n1, ml[0], ml[1], w, swa_tables, tm_lat)
            o_lat = _swa_attn(sink, ql, kl, vl, kc, vc, 512, 256)
            o_ctx = _swa_attn(sink, qc, None, None, kc, vc, n_ctx, n_ctx) if with_ctx_out else None
            w_o = swa_w_o[idx].astype(BF16)
        elif kind == 1:
            wdn = mla_w_down[idx]
            zc = jnp.zeros((d, MLA_NOPE), F32)
            wd_pad = jnp.concatenate([wdn[:, :Q_LORA + KV_LORA], zc, wdn[:, Q_LORA + KV_LORA:],
                                      zc[:, :LANES - MLA_QK]], axis=1).astype(BF16)
            wuq = jnp.pad(mla_w_uq[idx].reshape(Q_LORA, MLA_HEADS, MLA_QK),
                          ((0, 0), (0, 0), (0, LANES - MLA_QK))).reshape(Q_LORA, MLA_HEADS * LANES).astype(BF16)
            wuk = jnp.pad(mla_w_uk[idx].reshape(KV_LORA, MLA_HEADS, MLA_NOPE),
                          ((0, 0), (0, 0), (0, LANES - MLA_NOPE))).reshape(KV_LORA, MLA_HEADS * LANES).astype(BF16)
            gq = jnp.pad(mla_q_gain[idx] * (MLA_QK ** -0.5 * LOG2E), (0, LANES - MLA_QK))[None, :]
            gk = jnp.pad(mla_k_gain[idx], (0, LANES - MLA_QK))[None, :]
            w = (wd_pad, mla_q_lora_gain[idx][None, :], wuq, mla_kv_lora_gain[idx][None, :], wuk,
                 mla_w_uv[idx].astype(BF16), gq, gk)
            qc, kc, vc = _mla_proj(ctx, n1, mc[0], mc[1], w, None, tm_ctx)
            ql, kl, vl = _mla_proj(x, n1, ml[0], ml[1], w, mla_tables, tm_lat)
            o_lat = _mla_attn(ql, kl, vl, kc, vc, 1024, 256)
            o_ctx = _mla_attn(qc, None, None, kc, vc, n_ctx, n_ctx) if with_ctx_out else None
            w_o = mla_w_o[idx].astype(BF16)
        else:
            w_in = lru_w_in[idx].astype(BF16)
            cw = _pad_rows(jnp.concatenate([lru_conv_w[idx], lru_conv_b[idx][None, :]], axis=0), 8)
            gw = lru_gate_w[idx].reshape(2 * 2 * LRU_BLOCKS, LRU_BW, LRU_BW).astype(BF16)
            gb = _pad_rows(lru_gate_b[idx].reshape(4, d), 8)
            lam = _pad_rows(lru_lam[idx], 8)
            g_ctx, xr_ctx = _lru_in(ctx, n1, mc[0], mc[1], w_in, tm_ctx)
            g_lat, xr_lat = _lru_in(x, n1, ml[0], ml[1], w_in, tm_lat)
            afc, bfc, arc, brc = _lru_coef(xr_ctx, cw, gw, gb, lam, tm_ctx)
            afl, bfl, arl, brl = _lru_coef(xr_lat, cw, gw, gb, lam, tm_lat)
            h0 = jnp.zeros((bsz, d // LANES, LANES), F32)
            tc = 64
            sfc, hfc = _scan(afc, bfc, h0, None, False, tc)
            r_ctx, hrc = _scan(arc, brc, h0, sfc, True, tc)
            sfl, _ = _scan(afl, bfl, hfc, None, False, tc)
            r_lat, _ = _scan(arl, brl, hrc, sfl, True, tc)
            o_lat, o_ctx = g_lat, (g_ctx if with_ctx_out else None)
            w_o = lru_w_out[idx].astype(BF16)

        cw_ffn = _pad_rows(jnp.concatenate([ffn_conv_w[layer], ffn_conv_b[layer][None, :]], axis=0), 8)
        wf = (ffn_w_up[layer].astype(BF16), cw_ffn, ffn_w_down[layer].astype(BF16))

        x = _outproj(o_lat, r_lat, w_o, x, ml[2], tm_lat)
        x = _ffn(x, n2, ml[3], ml[4], ml[5], wf, tm_lat)
        if with_ctx_out:
            ctx = _outproj(o_ctx, r_ctx, w_o, ctx, mc[2], tm_ctx)
            ctx = _ffn(ctx, n2, mc[3], mc[4], mc[5], wf, tm_ctx)
    return x
```

```python
import functools

import jax
import jax.numpy as jnp
from jax import lax
from jax.experimental import pallas as pl
from jax.experimental.pallas import tpu as pltpu

F32 = jnp.float32
BF16 = jnp.bfloat16

GRID_W = 64
HEAD_DIM = 64
SWA_HEADS = 16
SWA_KV_HEADS = 4
WINDOW = 128
MLA_HEADS = 16
MLA_NOPE = 64
MLA_ROPE = 32
MLA_QK = MLA_NOPE + MLA_ROPE
Q_LORA = 384
KV_LORA = 256
LRU_BLOCKS = 4
LRU_BW = 256
LRU_C = 8.0
D_FF = 2816
ROPE_BASE = 10000.0
EPS = 1e-6
NEG_INF = -1e30
LOG2E = 1.4426950408889634

LANES = 128
MXU_COLS = 256
FF_CHUNK = MXU_COLS
HALO = 8
FFN_HALO = 16
MLA_KEY_BLOCK = 512
VMEM_LIMIT = 56 * 1024 * 1024


def _cparams(*sem):
    return pltpu.CompilerParams(dimension_semantics=sem, vmem_limit_bytes=VMEM_LIMIT)


def _const(shape):
    nd = len(shape)
    return pl.BlockSpec(shape, lambda *_: (0,) * nd, pipeline_mode=pl.Buffered(1))


def _rows(tm, d):
    return pl.BlockSpec((1, tm, d), lambda b, i: (b, i, 0))


def _vec(d):
    return pl.BlockSpec((1, 1, d), lambda b, i: (b, 0, 0))


def _dot(a, b):
    return jnp.dot(a, b, preferred_element_type=F32)


def _dot_t(a, b):
    return lax.dot_general(a, b, (((1,), (1,)), ((), ())), preferred_element_type=F32)


def _modnorm(x, gain, shift, scale):
    ms = jnp.mean(x * x, axis=-1, keepdims=True)
    y = x * lax.rsqrt(ms + EPS) * gain
    return y * (1.0 + scale) + shift


def _chunk_dot(x, m):
    w = m.shape[0]
    return jnp.concatenate([_dot(x[:, j * w:(j + 1) * w], m) for j in range(x.shape[1] // w)], axis=1)


def _lane_tile(t, width):
    return jnp.concatenate([t] * (width // t.shape[1]), axis=1)


def _norm_rope(x, group_ones, perm, cos, sin, eps_sum):
    ssq = _chunk_dot((x * x).astype(BF16), group_ones)
    y = x * _lane_tile(cos, x.shape[1])
    if sin is not None:
        y = y + _chunk_dot(x.astype(BF16), perm) * _lane_tile(sin, x.shape[1])
    return y * lax.rsqrt(ssq + eps_sum)


def _mod_body(c_ref, w_ref, b_ref, o_ref):
    c = c_ref[...]
    cs = (c * jax.nn.sigmoid(c)).astype(BF16)
    o_ref[0] = _dot(cs, w_ref[0].astype(BF16)) + b_ref[0]


def _mod_all(cond, mod_w, mod_b):
    depth, d, n = mod_w.shape
    tn = 1536
    rows = cond.shape[0]
    return pl.pallas_call(
        _mod_body,
        grid=(depth, n // tn),
        in_specs=[pl.BlockSpec((rows, d), lambda l, j: (0, 0)),
                  pl.BlockSpec((1, d, tn), lambda l, j: (l, 0, j)),
                  pl.BlockSpec((1, 1, tn), lambda l, j: (l, 0, j))],
        out_specs=pl.BlockSpec((1, rows, tn), lambda l, j: (l, 0, j)),
        out_shape=jax.ShapeDtypeStruct((depth, rows, n), F32),
        compiler_params=_cparams("arbitrary", "arbitrary"),
        name="mod",
    )(cond, mod_w, mod_b.reshape(depth, 1, n))


def _swa_proj_body(rope, x_ref, gain_ref, sh_ref, sc_ref, wq_ref, wk_ref, wv_ref, g64_ref, g128_ref,
                   perm_ref, cq_ref, ck_ref, *rest):
    if rope:
        sq_ref, sk_ref, q_out, k_out, v_out = rest
        sq, sk = sq_ref[...], sk_ref[...]
    else:
        q_out, k_out, v_out = rest
        sq = sk = None
    h = _modnorm(x_ref[0], gain_ref[...], sh_ref[0], sc_ref[0]).astype(BF16)
    q = _dot(h, wq_ref[...])
    k = _dot(h, wk_ref[...])
    v_out[0] = _dot(h, wv_ref[...]).astype(BF16)
    perm = perm_ref[...]
    q_out[0] = _norm_rope(q, g64_ref[...], perm, cq_ref[...], sq, HEAD_DIM * EPS).astype(BF16)
    k_out[0] = _norm_rope(k, g128_ref[...], perm, ck_ref[...], sk, 2 * HEAD_DIM * EPS).astype(BF16)


def _table_specs(tables, rope, tm):
    if rope:
        return [pl.BlockSpec((tm, LANES), lambda bb, i: (i, 0))] * len(tables)
    return [_const(t.shape) for t in tables]


def _swa_proj(x, gain, sh, sc, w, tables, rope, tm):
    b, l, d = x.shape
    wq, wk, wv = w[:3]
    ins = [x, gain, sh, sc] + list(w) + list(tables)
    specs = ([_rows(tm, d), _const(gain.shape), _vec(d), _vec(d)] + [_const(a.shape) for a in w]
             + _table_specs(tables, rope, tm))
    nq, nk = wq.shape[1], wk.shape[1]
    return pl.pallas_call(
        functools.partial(_swa_proj_body, rope),
        grid=(b, l // tm),
        in_specs=specs,
        out_specs=[_rows(tm, nq), _rows(tm, nk), _rows(tm, nk)],
        out_shape=[jax.ShapeDtypeStruct((b, l, nq), BF16), jax.ShapeDtypeStruct((b, l, nk), BF16),
                   jax.ShapeDtypeStruct((b, l, nk), BF16)],
        compiler_params=_cparams("parallel", "parallel"),
        name="swa_proj_lat" if rope else "swa_proj_ctx",
    )(*ins)


def _fold(x, op):
    acc = x[:, :LANES]
    for t in range(1, x.shape[1] // LANES):
        acc = op(acc, x[:, t * LANES:(t + 1) * LANES])
    return acc


def _attend(chains, s_scr):
    def phase1(ci):
        q, parts, _ = chains[ci]
        st = {"mx": None}

        def make(p, off):
            def step():
                k_thunk, _, bias, nk = parts[p]
                s = _dot_t(q, k_thunk())
                if bias is not None:
                    s = s + bias
                s_scr[ci % 2, :, off:off + nk] = s
                f = _fold(s, jnp.maximum)
                st["mx"] = f if st["mx"] is None else jnp.maximum(st["mx"], f)
            return step

        steps, off = [], 0
        for p, part in enumerate(parts):
            steps.append(make(p, off))
            off += part[3]
        return st, steps

    def phase2(ci, st):
        _, parts, sink = chains[ci]
        out = {"sum": None, "acc": [None, None]}

        def make(p, off):
            def step():
                if p == 0:
                    m = jnp.max(st["mx"], axis=-1, keepdims=True)
                    out["m"] = m if sink is None else jnp.maximum(m, sink)
                nk = parts[p][3]
                pr = jnp.exp2(s_scr[ci % 2, :, off:off + nk] - out["m"])
                f = _fold(pr, jnp.add)
                out["sum"] = f if out["sum"] is None else out["sum"] + f
                pv = _dot(pr.astype(BF16), parts[p][1]())
                out["acc"][p % 2] = pv if out["acc"][p % 2] is None else out["acc"][p % 2] + pv
            return step

        def finish():
            den = jnp.sum(out["sum"], axis=-1, keepdims=True)
            if sink is not None:
                den = den + jnp.exp2(sink - out["m"])
            acc = out["acc"][0] if out["acc"][1] is None else out["acc"][0] + out["acc"][1]
            return acc / den

        steps, off = [], 0
        for p, part in enumerate(parts):
            steps.append(make(p, off))
            off += part[3]
        return steps, finish

    results = []
    pending = None
    for ci in range(len(chains) + 1):
        st, steps1 = phase1(ci) if ci < len(chains) else (None, [])
        steps2 = pending[0] if pending else []
        for k in range(max(len(steps1), len(steps2))):
            if k < len(steps1):
                steps1[k]()
            if k < len(steps2):
                steps2[k]()
        if pending:
            results.append(pending[1]())
        pending = phase2(ci, st) if ci < len(chains) else None
    return results


def _swa_attn_body(has_win, tq, sub, seq, sink_ref, q_ref, *refs):
    if has_win:
        kd_ref, vd_ref, kc_ref, vc_ref, o_ref, s_scr = refs
    else:
        kc_ref, vc_ref, o_ref, s_scr = refs
    hk = pl.program_id(1)
    n_ctx = kc_ref.shape[1]
    span = sub + 2 * WINDOW
    low = lax.broadcasted_iota(jnp.int32, (sub, LANES), 1) < HEAD_DIM
    chains = []
    for r in range(tq // sub):
        if has_win:
            s0 = pl.program_id(2) * tq + r * sub
            ws = pl.multiple_of(jnp.clip(s0 - WINDOW, 0, seq - span), LANES)
            qpos = s0 + lax.broadcasted_iota(jnp.int32, (sub, span), 0)
            kpos = ws + lax.broadcasted_iota(jnp.int32, (sub, span), 1)
            bias = jnp.where(jnp.abs(qpos - kpos) <= WINDOW, 0.0, NEG_INF).astype(F32)
        for g in range(4):
            qp = q_ref[0, r * sub:(r + 1) * sub, (g // 2) * LANES:(g // 2 + 1) * LANES]
            qm = jnp.where(low if g % 2 == 0 else jnp.logical_not(low), qp, jnp.zeros_like(qp))
            parts = []
            if has_win:
                parts.append((lambda ws=ws: kd_ref[0, pl.ds(ws, span), :],
                              lambda ws=ws: vd_ref[0, pl.ds(ws, span), :], bias, span))
            parts.append((lambda: kc_ref[0], lambda: vc_ref[0], None, n_ctx))
            chains.append((qm, parts, sink_ref[hk * 4 + g]))
    o = _attend(chains, s_scr)
    for r in range(tq // sub):
        o_ref[0, r * sub:(r + 1) * sub, :] = jnp.concatenate(
            [jnp.where(low, o[4 * r], o[4 * r + 1]), jnp.where(low, o[4 * r + 2], o[4 * r + 3])],
            axis=1).astype(o_ref.dtype)


def _swa_attn(sink, q, kd, vd, kcd, vcd, tq, sub):
    b, l, dq = q.shape
    n_ctx = kcd.shape[1]
    has_win = kd is not None
    gw = dq // SWA_KV_HEADS
    ins = [sink, q]
    specs = [pl.BlockSpec(memory_space=pltpu.SMEM),
             pl.BlockSpec((1, tq, gw), lambda bb, h, i: (bb, i, h))]
    if has_win:
        ins += [kd, vd]
        specs += [pl.BlockSpec((1, l, LANES), lambda bb, h, i: (bb, 0, h))] * 2
    ins += [kcd, vcd]
    specs += [pl.BlockSpec((1, n_ctx, LANES), lambda bb, h, i: (bb, 0, h))] * 2
    return pl.pallas_call(
        functools.partial(_swa_attn_body, has_win, tq, sub, l),
        grid=(b, SWA_KV_HEADS, l // tq),
        in_specs=specs,
        out_specs=pl.BlockSpec((1, tq, gw), lambda bb, h, i: (bb, i, h)),
        out_shape=jax.ShapeDtypeStruct((b, l, dq), BF16),
        scratch_shapes=[pltpu.VMEM((2, sub, n_ctx + (sub + 2 * WINDOW if has_win else 0)), F32)],
        compiler_params=_cparams("parallel", "parallel", "parallel"),
        name="swa_attn_lat" if has_win else "swa_attn_ctx",
    )(*ins)


def _mla_proj_body(rope, x_ref, gain_ref, sh_ref, sc_ref, wd_ref, gql_ref, wuq_ref, gkvl_ref,
                   wuk_ref, wuv_ref, g128_ref, perm_ref, gkn_ref, cq_ref, ck_ref, *rest):
    if rope:
        sq_ref, sk_ref, q_out, k_out, v_out = rest
        sq, sk = sq_ref[...], sk_ref[...]
    else:
        q_out, k_out, v_out = rest
        sq = sk = None
    h = _modnorm(x_ref[0], gain_ref[...], sh_ref[0], sc_ref[0]).astype(BF16)
    t = _dot(h, wd_ref[...])
    cq = t[:, :Q_LORA]
    ckv = t[:, Q_LORA:Q_LORA + KV_LORA]
    kr = t[:, Q_LORA + KV_LORA:]
    cq = (cq * lax.rsqrt(jnp.mean(cq * cq, axis=-1, keepdims=True) + EPS) * gql_ref[...]).astype(BF16)
    ckv = (ckv * lax.rsqrt(jnp.mean(ckv * ckv, axis=-1, keepdims=True) + EPS) * gkvl_ref[...]).astype(BF16)
    qa = _dot(cq, wuq_ref[...])
    ka = _dot(ckv, wuk_ref[...])
    v_out[0] = _dot(ckv, wuv_ref[...]).astype(BF16)
    g128, perm = g128_ref[...], perm_ref[...]
    q = _norm_rope(qa, g128, perm, cq_ref[...], sq, MLA_QK * EPS).astype(BF16)
    kr_rot = kr * ck_ref[...]
    if rope:
        kr_rot = kr_rot + _dot(kr.astype(BF16), perm[:LANES, :LANES]) * sk
    width = ka.shape[1]
    kraw = ka + _lane_tile(kr, width)
    ssq = _chunk_dot((kraw * kraw).astype(BF16), g128)
    k = ((ka * _lane_tile(gkn_ref[...], width) + _lane_tile(kr_rot, width))
         * lax.rsqrt(ssq + MLA_QK * EPS)).astype(BF16)
    for hh in range(MLA_HEADS):
        q_out[0, hh] = q[:, hh * LANES:(hh + 1) * LANES]
        k_out[0, hh] = k[:, hh * LANES:(hh + 1) * LANES]


def _mla_proj(x, gain, sh, sc, w, tables, rope, tm):
    b, l, d = x.shape
    ins = [x, gain, sh, sc] + list(w) + list(tables)
    specs = ([_rows(tm, d), _const(gain.shape), _vec(d), _vec(d)] + [_const(a.shape) for a in w]
             + _table_specs(tables, rope, tm))
    hspec = pl.BlockSpec((1, MLA_HEADS, tm, LANES), lambda bb, i: (bb, 0, i, 0))
    hshape = jax.ShapeDtypeStruct((b, MLA_HEADS, l, LANES), BF16)
    nv = w[5].shape[1]
    return pl.pallas_call(
        functools.partial(_mla_proj_body, rope),
        grid=(b, l // tm),
        in_specs=specs,
        out_specs=[hspec, hspec, _rows(tm, nv)],
        out_shape=[hshape, hshape, jax.ShapeDtypeStruct((b, l, nv), BF16)],
        compiler_params=_cparams("parallel", "parallel"),
        name="mla_proj_lat" if rope else "mla_proj_ctx",
    )(*ins)


def _mla_attn_body(has_lat, tk, sub, q_ref, *refs):
    if has_lat:
        kl_ref, vl_ref, kc_ref, vc_ref, o_ref, s_scr = refs
    else:
        kc_ref, vc_ref, o_ref, s_scr = refs
    tq = q_ref.shape[2]
    n_ctx = kc_ref.shape[2]
    low = lax.broadcasted_iota(jnp.int32, (sub, LANES), 1) < MLA_NOPE
    chains = []
    for r in range(tq // sub):
        for e in range(2):
            parts = []
            if has_lat:
                for j in range(kl_ref.shape[2] // tk):
                    parts.append((lambda e=e, j=j: kl_ref[0, e, j * tk:(j + 1) * tk, :],
                                  lambda j=j: vl_ref[0, j * tk:(j + 1) * tk, :], None, tk))
            parts.append((lambda e=e: kc_ref[0, e], lambda: vc_ref[0], None, n_ctx))
            chains.append((q_ref[0, e, r * sub:(r + 1) * sub, :], parts, None))
    o = _attend(chains, s_scr)
    for r in range(tq // sub):
        o_ref[0, r * sub:(r + 1) * sub, :] = jnp.where(low, o[2 * r], o[2 * r + 1]).astype(o_ref.dtype)


def _mla_attn(q, kl, vl, kc, vc, tq, sub):
    b, nh, l, _ = q.shape
    n_ctx = kc.shape[2]
    has_lat = kl is not None
    ins = [q]
    specs = [pl.BlockSpec((1, 2, tq, LANES), lambda bb, j, i: (bb, j, i, 0))]
    if has_lat:
        ins += [kl, vl]
        specs += [pl.BlockSpec((1, 2, l, LANES), lambda bb, j, i: (bb, j, 0, 0)),
                  pl.BlockSpec((1, l, LANES), lambda bb, j, i: (bb, 0, j))]
    ins += [kc, vc]
    specs += [pl.BlockSpec((1, 2, n_ctx, LANES), lambda bb, j, i: (bb, j, 0, 0)),
              pl.BlockSpec((1, n_ctx, LANES), lambda bb, j, i: (bb, 0, j))]
    return pl.pallas_call(
        functools.partial(_mla_attn_body, has_lat, MLA_KEY_BLOCK, sub),
        grid=(b, nh // 2, l // tq),
        in_specs=specs,
        out_specs=pl.BlockSpec((1, tq, LANES), lambda bb, j, i: (bb, i, j)),
        out_shape=jax.ShapeDtypeStruct((b, l, nh * MLA_NOPE), BF16),
        scratch_shapes=[pltpu.VMEM((2, sub, n_ctx + (l if has_lat else 0)), F32)],
        compiler_params=_cparams("parallel", "parallel", "parallel"),
        name="mla_attn_lat" if has_lat else "mla_attn_ctx",
    )(*ins)


def _lru_in_body(x_ref, gain_ref, sh_ref, sc_ref, w_ref, g_out, x_out):
    h = _modnorm(x_ref[0], gain_ref[...], sh_ref[0], sc_ref[0]).astype(BF16)
    u = _dot(h, w_ref[...])
    width = u.shape[1] // 2
    g_out[0] = jax.nn.gelu(u[:, :width], approximate=True).astype(BF16)
    x_out[0] = u[:, width:]


def _lru_in(x, gain, sh, sc, w_in, tm):
    b, l, d = x.shape
    width = w_in.shape[1] // 2
    return pl.pallas_call(
        _lru_in_body,
        grid=(b, l // tm),
        in_specs=[_rows(tm, d), _const(gain.shape), _vec(d), _vec(d), _const(w_in.shape)],
        out_specs=[_rows(tm, width), _rows(tm, width)],
        out_shape=[jax.ShapeDtypeStruct((b, l, width), BF16), jax.ShapeDtypeStruct((b, l, width), F32)],
        compiler_params=_cparams("parallel", "parallel"),
        name="lru_in",
    )(x, gain, sh, sc, w_in)


def _halo_specs(tm, l, d, halo=HALO):
    nb = tm // halo
    last = l // halo - 1
    prev = pl.BlockSpec((1, halo, d), lambda b, i: (b, jnp.maximum(i * nb - 1, 0), 0))
    nxt = pl.BlockSpec((1, halo, d), lambda b, i: (b, jnp.minimum((i + 1) * nb, last), 0))
    return prev, nxt


def _lru_coef_body(x_ref, xp_ref, xn_ref, cw_ref, gw_ref, gb_ref, lam_ref, af_out, bf_out, ar_out, br_out):
    i = pl.program_id(1)
    tm = x_ref.shape[1]
    x = x_ref[0]
    pm = (i > 0).astype(F32)
    nm = (i < pl.num_programs(1) - 1).astype(F32)
    p2 = xp_ref[0, HALO - 2:HALO - 1, :] * pm
    p1 = xp_ref[0, HALO - 1:HALO, :] * pm
    n1 = xn_ref[0, 0:1, :] * nm
    row = lax.broadcasted_iota(jnp.int32, x.shape, 0)
    xm1 = jnp.where(row == 0, p1, pltpu.roll(x, 1, 0))
    xm2 = jnp.where(row == 0, p2, jnp.where(row == 1, p1, pltpu.roll(x, 2, 0)))
    xp1 = jnp.where(row == tm - 1, n1, pltpu.roll(x, tm - 1, 0))
    cw = cw_ref[...]
    xc = cw[0:1] * xm2 + cw[1:2] * xm1 + cw[2:3] * x + cw[3:4] * xp1 + cw[4:5]
    xcb = xc.astype(BF16)
    outs = ((af_out, bf_out), (ar_out, br_out))
    for d in range(2):
        gates = []
        for k in range(2):
            g = jnp.concatenate(
                [_dot(xcb[:, n * LRU_BW:(n + 1) * LRU_BW], gw_ref[(d * 2 + k) * LRU_BLOCKS + n])
                 for n in range(LRU_BLOCKS)], axis=1)
            gates.append(jax.nn.sigmoid(g + gb_ref[d * 2 + k:d * 2 + k + 1, :]))
        r, ig = gates
        nl = -lam_ref[d:d + 1, :]
        softplus = jnp.maximum(nl, 0.0) + jnp.log1p(jnp.exp(-jnp.abs(nl)))
        log_a = (-LRU_C) * r * softplus
        a = jnp.exp(log_a)
        outs[d][0][0] = a
        outs[d][1][0] = jnp.sqrt(-jnp.tanh(log_a) * (a * a + 1.0)) * (ig * xc)


def _lru_coef(xr, cw, gw, gb, lam, tm):
    b, l, d = xr.shape
    prev, nxt = _halo_specs(tm, l, d)
    out = jax.ShapeDtypeStruct((b, l, d), F32)
    return pl.pallas_call(
        _lru_coef_body,
        grid=(b, l // tm),
        in_specs=[_rows(tm, d), prev, nxt, _const(cw.shape), _const(gw.shape), _const(gb.shape),
                  _const(lam.shape)],
        out_specs=[_rows(tm, d)] * 4,
        out_shape=[out] * 4,
        compiler_params=_cparams("parallel", "parallel"),
        name="lru_coef",
    )(xr, xr, xr, cw, gw, gb, lam)


def _scan_body(reverse, has_add, a_ref, b_ref, h0_ref, *rest):
    if has_add:
        add_ref, o_ref, he_ref, h_s = rest
    else:
        o_ref, he_ref, h_s = rest
    nb, tc = a_ref.shape[0], a_ref.shape[1]

    @pl.when(pl.program_id(0) == 0)
    def _():
        h_s[...] = h0_ref[...]

    def step(j, hc):
        t = tc - 1 - j if reverse else j
        new = []
        for bb in range(nb):
            hn = a_ref[bb, t] * hc[bb] + b_ref[bb, t]
            o_ref[bb, t] = hn + add_ref[bb, t] if has_add else hn
            new.append(hn)
        return tuple(new)

    hc = lax.fori_loop(0, tc, step, tuple(h_s[bb] for bb in range(nb)), unroll=4)
    for bb in range(nb):
        h_s[bb] = hc[bb]
        he_ref[bb] = hc[bb]


def _scan(a, b, h0, add, reverse, tc):
    nb, t, d = a.shape
    sub = d // LANES
    shp = (nb, t, sub, LANES)
    nt = t // tc
    imap = (lambda i: (0, nt - 1 - i, 0, 0)) if reverse else (lambda i: (0, i, 0, 0))
    blk = pl.BlockSpec((nb, tc, sub, LANES), imap)
    hspec = pl.BlockSpec((nb, sub, LANES), lambda i: (0, 0, 0))
    ins = [a.reshape(shp), b.reshape(shp), h0]
    specs = [blk, blk, hspec]
    if add is not None:
        ins.append(add.reshape(shp))
        specs.append(blk)
    out, h_end = pl.pallas_call(
        functools.partial(_scan_body, reverse, add is not None),
        grid=(nt,),
        in_specs=specs,
        out_specs=[blk, hspec],
        out_shape=[jax.ShapeDtypeStruct(shp, F32), jax.ShapeDtypeStruct((nb, sub, LANES), F32)],
        scratch_shapes=[pltpu.VMEM((nb, sub, LANES), F32)],
        compiler_params=_cparams("arbitrary"),
        name="lru_scan_rev" if reverse else "lru_scan_fwd",
    )(*ins)
    return out.reshape(nb, t, d), h_end


def _block_tail_body(has_r, x_ref, xp_ref, xn_ref, o_ref, op_ref, on_ref, *rest):
    if has_r:
        r_ref, rp_ref, rn_ref = rest[:3]
        rest = rest[3:]
    (wo_ref, g1_ref, gain_ref, sh_ref, sc_ref, g2_ref, wu_ref, cw_ref, wd_ref, y_ref, x1_s, h_s, act_s) = rest
    i = pl.program_id(1)
    tm = x_ref.shape[1]
    d_ff = wd_ref.shape[0]
    gain, sh, sc, g1 = gain_ref[...], sh_ref[0], sc_ref[0], g1_ref[0]

    def mixed(o, r):
        return o if r is None else (o.astype(F32) * r).astype(BF16)

    wo = wo_ref[...]
    x1 = x_ref[0] + g1 * _dot(mixed(o_ref[0], r_ref[0] if has_r else None), wo)
    x1_s[...] = x1
    h_s[0:tm, :] = _modnorm(x1, gain, sh, sc).astype(BF16)
    o_halo = jnp.concatenate([op_ref[0], on_ref[0]], axis=0)
    r_halo = jnp.concatenate([rp_ref[0], rn_ref[0]], axis=0) if has_r else None
    x_halo = jnp.concatenate([xp_ref[0], xn_ref[0]], axis=0)
    x1_halo = x_halo + g1 * _dot(mixed(o_halo, r_halo), wo)
    h_s[tm:tm + 2 * FFN_HALO, :] = _modnorm(x1_halo, gain, sh, sc).astype(BF16)
    pm = (i > 0).astype(F32)
    nm = (i < pl.num_programs(1) - 1).astype(F32)

    def up(c):
        lo = c * FF_CHUNK
        return (_dot(h_s[...], wu_ref[:, lo:lo + FF_CHUNK]),
                _dot(h_s[0:tm, :], wu_ref[:, d_ff + lo:d_ff + lo + FF_CHUNK]))

    nxt = up(0)
    for c in range(d_ff // FF_CHUNK):
        ge, val = nxt
        if (c + 1) * FF_CHUNK < d_ff:
            nxt = up(c + 1)
        lo = c * FF_CHUNK
        g = ge[0:tm]
        gp = ge[tm + FFN_HALO - 1:tm + FFN_HALO] * pm
        gn = ge[tm + FFN_HALO:tm + FFN_HALO + 1] * nm
        row = lax.broadcasted_iota(jnp.int32, g.shape, 0)
        gd = jnp.where(row == 0, gp, pltpu.roll(g, 1, 0))
        gu = jnp.where(row == tm - 1, gn, pltpu.roll(g, tm - 1, 0))
        cw = cw_ref[:, lo:lo + FF_CHUNK]
        y = cw[0:1] * gd + cw[1:2] * g + cw[2:3] * gu + cw[3:4]
        act_s[:, lo:lo + FF_CHUNK] = (y * jax.nn.sigmoid(y) * val).astype(BF16)
    y_ref[0] = x1_s[...] + g2_ref[0] * _dot(act_s[...], wd_ref[...])


def _block_tail(x, o, r, w_o, g1, gain, sh, sc, g2, w, tm):
    b, l, d = x.shape
    dk = o.shape[2]
    wu, cw, wd = w
    xs = [_rows(tm, d), *_halo_specs(tm, l, d, FFN_HALO)]
    os_ = [_rows(tm, dk), *_halo_specs(tm, l, dk, FFN_HALO)]
    ins = [x, x, x, o, o, o] + ([r, r, r] if r is not None else [])
    specs = xs + os_ + (os_ if r is not None else [])
    ins += [w_o, g1, gain, sh, sc, g2, wu, cw, wd]
    specs += [_const(w_o.shape), _vec(d), _const(gain.shape), _vec(d), _vec(d), _vec(d),
              _const(wu.shape), _const(cw.shape), _const(wd.shape)]
    return pl.pallas_call(
        functools.partial(_block_tail_body, r is not None),
        grid=(b, l // tm),
        in_specs=specs,
        out_specs=_rows(tm, d),
        out_shape=jax.ShapeDtypeStruct((b, l, d), F32),
        scratch_shapes=[pltpu.VMEM((tm, d), F32), pltpu.VMEM((tm + 2 * FFN_HALO, d), BF16),
                        pltpu.VMEM((tm, wd.shape[0]), BF16)],
        compiler_params=_cparams("parallel", "parallel"),
        name="block_tail",
    )(*ins)


def _rope_cs(n_tokens, rot_dim):
    pos = jnp.arange(n_tokens)
    row = (pos // GRID_W).astype(F32)
    col = (pos % GRID_W).astype(F32)
    n_freq = rot_dim // 4
    inv = ROPE_BASE ** (-jnp.arange(n_freq, dtype=F32) / n_freq)
    ar, ac = row[:, None] * inv, col[:, None] * inv
    cos = jnp.concatenate([jnp.cos(ar), jnp.cos(ar), jnp.cos(ac), jnp.cos(ac)], axis=1)
    sin = jnp.concatenate([-jnp.sin(ar), jnp.sin(ar), -jnp.sin(ac), jnp.sin(ac)], axis=1)
    return cos, sin


def _pad_rows(a, rows):
    return jnp.pad(a, ((0, rows - a.shape[0]), (0, 0)))


def _pair_swap(n, half):
    i = jnp.arange(n)
    return jnp.where((i & half) == 0, i + half, i - half)


def _perm_matrix(half):
    return (jnp.arange(MXU_COLS)[:, None] == _pair_swap(MXU_COLS, half)[None, :]).astype(BF16)


def _group_ones(size):
    g = jnp.arange(MXU_COLS) // size
    return (g[:, None] == g[None, :]).astype(BF16)


def _gain_tables(gain, cos, sin, half):
    return gain[None, :] * cos, gain[_pair_swap(LANES, half)][None, :] * sin, gain[None, :]


def kernel(x, c, ctx, c_ctx, norm1, norm2, mod_w, mod_b, swa_w_qkv, swa_q_gain, swa_k_gain, swa_sink, swa_w_o, mla_w_down, mla_q_lora_gain, mla_w_uq, mla_kv_lora_gain, mla_w_uk, mla_w_uv, mla_q_gain, mla_k_gain, mla_w_o, lru_w_in, lru_conv_w, lru_conv_b, lru_gate_w, lru_gate_b, lru_lam, lru_w_out, ffn_w_up, ffn_conv_w, ffn_conv_b, ffn_w_down):
    bsz, seq, d = x.shape
    n_ctx = ctx.shape[1]
    depth = norm1.shape[0]
    tm_lat, tm_ctx = 512, n_ctx

    cond = _pad_rows(jnp.concatenate([c, c_ctx[None, :]], axis=0), 16)
    mods = _mod_all(cond, mod_w, mod_b)

    cos_s, sin_s = _rope_cs(seq, HEAD_DIM)
    cos_s, sin_s = jnp.tile(cos_s, (1, 2)), jnp.tile(sin_s, (1, 2))
    cos_m, sin_m = _rope_cs(seq, MLA_ROPE)
    one = jnp.ones((seq, MLA_NOPE), F32)
    zero = jnp.zeros((seq, MLA_NOPE), F32)
    cos_m = jnp.concatenate([one, cos_m, one[:, :LANES - MLA_QK]], axis=1)
    sin_m = jnp.concatenate([zero, sin_m, zero[:, :LANES - MLA_QK]], axis=1)
    ones64, ones128 = _group_ones(HEAD_DIM), _group_ones(LANES)

    for layer in range(depth):
        kind, idx = layer % 3, layer // 3
        with_ctx_out = layer < depth - 1
        m = mods[layer].reshape(16, 6, d)
        ml = [m[:bsz, j][:, None, :] for j in range(6)]
        mc = [jnp.broadcast_to(m[bsz, j][None, None, :], (bsz, 1, d)) for j in range(6)]
        n1 = norm1[layer][None, :]
        n2 = norm2[layer][None, :]
        r_lat = r_ctx = None
        if kind == 0:
            wqkv = swa_w_qkv[idx]
            nq = SWA_HEADS * HEAD_DIM
            nkv = SWA_KV_HEADS * HEAD_DIM

            def dup(w):
                w = w.reshape(d, SWA_KV_HEADS, 1, HEAD_DIM)
                return jnp.broadcast_to(w, (d, SWA_KV_HEADS, 2, HEAD_DIM)).reshape(d, 2 * nkv).astype(BF16)

            w = (wqkv[:, :nq].astype(BF16), dup(wqkv[:, nq:nq + nkv]), dup(wqkv[:, nq + nkv:]),
                 ones64, ones128, _perm_matrix(HEAD_DIM // 4))
            cq, sq, gq = _gain_tables(jnp.tile(swa_q_gain[idx], 2) * LOG2E, cos_s, sin_s, HEAD_DIM // 4)
            ck, sk, gk = _gain_tables(jnp.tile(swa_k_gain[idx], 2) * (2 * HEAD_DIM) ** 0.5, cos_s, sin_s,
                                      HEAD_DIM // 4)
            sink = swa_sink[idx] * LOG2E
            qc, kc, vc = _swa_proj(ctx, n1, mc[0], mc[1], w, (gq, gk), False, tm_ctx)
            ql, kl, vl = _swa_proj(x, n1, ml[0], ml[1], w, (cq, ck, sq, sk), True, tm_lat)
            o_lat = _swa_attn(sink, ql, kl, vl, kc, vc, 512, 256)
            o_ctx = _swa_attn(sink, qc, None, None, kc, vc, n_ctx, n_ctx) if with_ctx_out else None
            w_o = swa_w_o[idx].astype(BF16)
        elif kind == 1:
            wdn = mla_w_down[idx]
            zc = jnp.zeros((d, MLA_NOPE), F32)
            wd_pad = jnp.concatenate([wdn[:, :Q_LORA + KV_LORA], zc, wdn[:, Q_LORA + KV_LORA:],
                                      zc[:, :LANES - MLA_QK]], axis=1).astype(BF16)
            wuq = jnp.pad(mla_w_uq[idx].reshape(Q_LORA, MLA_HEADS, MLA_QK),
                          ((0, 0), (0, 0), (0, LANES - MLA_QK))).reshape(Q_LORA, MLA_HEADS * LANES).astype(BF16)
            wuk = jnp.pad(mla_w_uk[idx].reshape(KV_LORA, MLA_HEADS, MLA_NOPE),
                          ((0, 0), (0, 0), (0, LANES - MLA_NOPE))).reshape(KV_LORA, MLA_HEADS * LANES).astype(BF16)
            root = MLA_QK ** 0.5
            gq = jnp.pad(mla_q_gain[idx] * LOG2E, (0, LANES - MLA_QK))
            gk_nope = jnp.pad(mla_k_gain[idx][:MLA_NOPE] * root, (0, LANES - MLA_NOPE))[None, :]
            gk_rope = jnp.pad(mla_k_gain[idx][MLA_NOPE:] * root, (MLA_NOPE, LANES - MLA_QK))
            cq, sq, gq = _gain_tables(gq, cos_m, sin_m, MLA_ROPE // 4)
            ck, sk, gk = _gain_tables(gk_rope, cos_m, sin_m, MLA_ROPE // 4)
            w = (wd_pad, mla_q_lora_gain[idx][None, :], wuq, mla_kv_lora_gain[idx][None, :], wuk,
                 mla_w_uv[idx].astype(BF16), ones128, _perm_matrix(MLA_ROPE // 4), gk_nope)
            qc, kc, vc = _mla_proj(ctx, n1, mc[0], mc[1], w, (gq, gk), False, tm_ctx)
            ql, kl, vl = _mla_proj(x, n1, ml[0], ml[1], w, (cq, ck, sq, sk), True, tm_lat)
            o_lat = _mla_attn(ql, kl, vl, kc, vc, 1024, 256)
            o_ctx = _mla_attn(qc, None, None, kc, vc, n_ctx, n_ctx) if with_ctx_out else None
            w_o = mla_w_o[idx].astype(BF16)
        else:
            w_in = lru_w_in[idx].astype(BF16)
            cw = _pad_rows(jnp.concatenate([lru_conv_w[idx], lru_conv_b[idx][None, :]], axis=0), 8)
            gw = lru_gate_w[idx].reshape(2 * 2 * LRU_BLOCKS, LRU_BW, LRU_BW).astype(BF16)
            gb = _pad_rows(lru_gate_b[idx].reshape(4, d), 8)
            lam = _pad_rows(lru_lam[idx], 8)
            g_ctx, xr_ctx = _lru_in(ctx, n1, mc[0], mc[1], w_in, tm_ctx)
            g_lat, xr_lat = _lru_in(x, n1, ml[0], ml[1], w_in, tm_lat)
            afc, bfc, arc, brc = _lru_coef(xr_ctx, cw, gw, gb, lam, tm_ctx)
            afl, bfl, arl, brl = _lru_coef(xr_lat, cw, gw, gb, lam, tm_lat)
            h0 = jnp.zeros((bsz, d // LANES, LANES), F32)
            tc = 64
            sfc, hfc = _scan(afc, bfc, h0, None, False, tc)
            r_ctx, hrc = _scan(arc, brc, h0, sfc, True, tc)
            sfl, _ = _scan(afl, bfl, hfc, None, False, tc)
            r_lat, _ = _scan(arl, brl, hrc, sfl, True, tc)
            o_lat, o_ctx = g_lat, (g_ctx if with_ctx_out else None)
            w_o = lru_w_out[idx].astype(BF16)

        cw_ffn = _pad_rows(jnp.concatenate([ffn_conv_w[layer], ffn_conv_b[layer][None, :]], axis=0), 8)
        wf = (ffn_w_up[layer].astype(BF16), cw_ffn, ffn_w_down[layer].astype(BF16))

        x = _block_tail(x, o_lat, r_lat, w_o, ml[2], n2, ml[3], ml[4], ml[5], wf, tm_lat)
        if with_ctx_out:
            ctx = _block_tail(ctx, o_ctx, r_ctx, w_o, mc[2], n2, mc[3], mc[4], mc[5], wf, tm_ctx)
    return x
```

```python
import functools

import jax
import jax.numpy as jnp
from jax import lax
from jax.experimental import pallas as pl
from jax.experimental.pallas import tpu as pltpu

F32 = jnp.float32
BF16 = jnp.bfloat16

GRID_W = 64
HEAD_DIM = 64
SWA_HEADS = 16
SWA_KV_HEADS = 4
WINDOW = 128
MLA_HEADS = 16
MLA_NOPE = 64
MLA_ROPE = 32
MLA_QK = MLA_NOPE + MLA_ROPE
Q_LORA = 384
KV_LORA = 256
LRU_BLOCKS = 4
LRU_BW = 256
LRU_C = 8.0
D_FF = 2816
ROPE_BASE = 10000.0
EPS = 1e-6
NEG_INF = -1e30
LOG2E = 1.4426950408889634

LANES = 128
MXU_COLS = 256
FF_CHUNK = MXU_COLS
SUBLANES = 8
HALO = SUBLANES
FFN_HALO = 16
MLA_KEY_BLOCK = 512
VMEM_LIMIT = 56 * 1024 * 1024


def _cparams(*sem):
    return pltpu.CompilerParams(dimension_semantics=sem, vmem_limit_bytes=VMEM_LIMIT)


def _const(shape):
    nd = len(shape)
    return pl.BlockSpec(shape, lambda *_: (0,) * nd, pipeline_mode=pl.Buffered(1))


def _rows(tm, d):
    return pl.BlockSpec((1, tm, d), lambda b, i: (b, i, 0))


def _vec(d):
    return pl.BlockSpec((1, 1, d), lambda b, i: (b, 0, 0))


def _dot(a, b):
    return jnp.dot(a, b, preferred_element_type=F32)


def _dot_t(a, b):
    return lax.dot_general(a, b, (((1,), (1,)), ((), ())), preferred_element_type=F32)


def _modnorm(x, gain, shift, scale):
    ms = jnp.mean(x * x, axis=-1, keepdims=True)
    y = x * lax.rsqrt(ms + EPS) * gain
    return y * (1.0 + scale) + shift


def _chunk_dot(x, m):
    w = m.shape[0]
    return jnp.concatenate([_dot(x[:, j * w:(j + 1) * w], m) for j in range(x.shape[1] // w)], axis=1)


def _lane_tile(t, width):
    return jnp.concatenate([t] * (width // t.shape[1]), axis=1)


def _norm_rope(x, group_ones, perm, cos, sin, eps_sum):
    ssq = _chunk_dot((x * x).astype(BF16), group_ones)
    y = x * _lane_tile(cos, x.shape[1])
    if sin is not None:
        y = y + _chunk_dot(x.astype(BF16), perm) * _lane_tile(sin, x.shape[1])
    return y * lax.rsqrt(ssq + eps_sum)


def _mod_body(c_ref, w_ref, b_ref, o_ref):
    c = c_ref[...]
    cs = (c * jax.nn.sigmoid(c)).astype(BF16)
    o_ref[0] = _dot(cs, w_ref[0].astype(BF16)) + b_ref[0]


def _mod_all(cond, mod_w, mod_b):
    depth, d, n = mod_w.shape
    tn = 1536
    rows = cond.shape[0]
    return pl.pallas_call(
        _mod_body,
        grid=(depth, n // tn),
        in_specs=[pl.BlockSpec((rows, d), lambda l, j: (0, 0)),
                  pl.BlockSpec((1, d, tn), lambda l, j: (l, 0, j)),
                  pl.BlockSpec((1, 1, tn), lambda l, j: (l, 0, j))],
        out_specs=pl.BlockSpec((1, rows, tn), lambda l, j: (l, 0, j)),
        out_shape=jax.ShapeDtypeStruct((depth, rows, n), F32),
        compiler_params=_cparams("arbitrary", "arbitrary"),
        name="mod",
    )(cond, mod_w, mod_b.reshape(depth, 1, n))


def _swa_proj_body(rope, x_ref, gain_ref, sh_ref, sc_ref, wq_ref, wk_ref, wv_ref, g64_ref, g128_ref,
                   perm_ref, cq_ref, ck_ref, *rest):
    if rope:
        sq_ref, sk_ref, q_out, k_out, v_out = rest
        sq, sk = sq_ref[...], sk_ref[...]
    else:
        q_out, k_out, v_out = rest
        sq = sk = None
    h = _modnorm(x_ref[0], gain_ref[...], sh_ref[0], sc_ref[0]).astype(BF16)
    q = _dot(h, wq_ref[...])
    k = _dot(h, wk_ref[...])
    v_out[0] = _dot(h, wv_ref[...]).astype(BF16)
    perm = perm_ref[...]
    q_out[0] = _norm_rope(q, g64_ref[...], perm, cq_ref[...], sq, HEAD_DIM * EPS).astype(BF16)
    k_out[0] = _norm_rope(k, g128_ref[...], perm, ck_ref[...], sk, 2 * HEAD_DIM * EPS).astype(BF16)


def _table_specs(tables, rope, tm):
    if rope:
        return [pl.BlockSpec((tm, LANES), lambda bb, i: (i, 0))] * len(tables)
    return [_const(t.shape) for t in tables]


def _swa_proj(x, gain, sh, sc, w, tables, rope, tm):
    b, l, d = x.shape
    wq, wk, wv = w[:3]
    ins = [x, gain, sh, sc] + list(w) + list(tables)
    specs = ([_rows(tm, d), _const(gain.shape), _vec(d), _vec(d)] + [_const(a.shape) for a in w]
             + _table_specs(tables, rope, tm))
    nq, nk = wq.shape[1], wk.shape[1]
    return pl.pallas_call(
        functools.partial(_swa_proj_body, rope),
        grid=(b, l // tm),
        in_specs=specs,
        out_specs=[_rows(tm, nq), _rows(tm, nk), _rows(tm, nk)],
        out_shape=[jax.ShapeDtypeStruct((b, l, nq), BF16), jax.ShapeDtypeStruct((b, l, nk), BF16),
                   jax.ShapeDtypeStruct((b, l, nk), BF16)],
        compiler_params=_cparams("parallel", "parallel"),
        name="swa_proj_lat" if rope else "swa_proj_ctx",
    )(*ins)


def _fold(x, op):
    acc = x[:, :LANES]
    for t in range(1, x.shape[1] // LANES):
        acc = op(acc, x[:, t * LANES:(t + 1) * LANES])
    return acc


def _attend(chains, s_scr):
    def phase1(ci):
        q, parts, _ = chains[ci]
        st = {"mx": None}

        def make(p, off):
            def step():
                k_thunk, _, bias, nk = parts[p]
                s = _dot_t(q, k_thunk())
                if bias is not None:
                    s = s + bias
                s_scr[ci % 2, :, off:off + nk] = s
                f = _fold(s, jnp.maximum)
                st["mx"] = f if st["mx"] is None else jnp.maximum(st["mx"], f)
            return step

        steps, off = [], 0
        for p, part in enumerate(parts):
            steps.append(make(p, off))
            off += part[3]
        return st, steps

    def phase2(ci, st):
        _, parts, sink = chains[ci]
        out = {"sum": None, "acc": [None, None]}

        def make(p, off):
            def step():
                if p == 0:
                    m = jnp.max(st["mx"], axis=-1, keepdims=True)
                    out["m"] = m if sink is None else jnp.maximum(m, sink)
                nk = parts[p][3]
                pr = jnp.exp2(s_scr[ci % 2, :, off:off + nk] - out["m"])
                f = _fold(pr, jnp.add)
                out["sum"] = f if out["sum"] is None else out["sum"] + f
                pv = _dot(pr.astype(BF16), parts[p][1]())
                out["acc"][p % 2] = pv if out["acc"][p % 2] is None else out["acc"][p % 2] + pv
            return step

        def finish():
            den = jnp.sum(out["sum"], axis=-1, keepdims=True)
            if sink is not None:
                den = den + jnp.exp2(sink - out["m"])
            acc = out["acc"][0] if out["acc"][1] is None else out["acc"][0] + out["acc"][1]
            return acc / den

        steps, off = [], 0
        for p, part in enumerate(parts):
            steps.append(make(p, off))
            off += part[3]
        return steps, finish

    results = []
    pending = None
    for ci in range(len(chains) + 1):
        st, steps1 = phase1(ci) if ci < len(chains) else (None, [])
        steps2 = pending[0] if pending else []
        for k in range(max(len(steps1), len(steps2))):
            if k < len(steps1):
                steps1[k]()
            if k < len(steps2):
                steps2[k]()
        if pending:
            results.append(pending[1]())
        pending = phase2(ci, st) if ci < len(chains) else None
    return results


def _swa_attn_body(has_win, tq, sub, seq, sink_ref, q_ref, *refs):
    if has_win:
        kd_ref, vd_ref, kc_ref, vc_ref, o_ref, s_scr = refs
    else:
        kc_ref, vc_ref, o_ref, s_scr = refs
    hk = pl.program_id(1)
    n_ctx = kc_ref.shape[1]
    span = sub + 2 * WINDOW
    low = lax.broadcasted_iota(jnp.int32, (sub, LANES), 1) < HEAD_DIM
    chains = []
    for r in range(tq // sub):
        if has_win:
            s0 = pl.program_id(2) * tq + r * sub
            ws = pl.multiple_of(jnp.clip(s0 - WINDOW, 0, seq - span), LANES)
            qpos = s0 + lax.broadcasted_iota(jnp.int32, (sub, span), 0)
            kpos = ws + lax.broadcasted_iota(jnp.int32, (sub, span), 1)
            bias = jnp.where(jnp.abs(qpos - kpos) <= WINDOW, 0.0, NEG_INF).astype(F32)
        for g in range(4):
            qp = q_ref[0, r * sub:(r + 1) * sub, (g // 2) * LANES:(g // 2 + 1) * LANES]
            qm = jnp.where(low if g % 2 == 0 else jnp.logical_not(low), qp, jnp.zeros_like(qp))
            parts = []
            if has_win:
                parts.append((lambda ws=ws: kd_ref[0, pl.ds(ws, span), :],
                              lambda ws=ws: vd_ref[0, pl.ds(ws, span), :], bias, span))
            parts.append((lambda: kc_ref[0], lambda: vc_ref[0], None, n_ctx))
            chains.append((qm, parts, sink_ref[hk * 4 + g]))
    o = _attend(chains, s_scr)
    for r in range(tq // sub):
        o_ref[0, r * sub:(r + 1) * sub, :] = jnp.concatenate(
            [jnp.where(low, o[4 * r], o[4 * r + 1]), jnp.where(low, o[4 * r + 2], o[4 * r + 3])],
            axis=1).astype(o_ref.dtype)


def _swa_attn(sink, q, kd, vd, kcd, vcd, tq, sub):
    b, l, dq = q.shape
    n_ctx = kcd.shape[1]
    has_win = kd is not None
    gw = dq // SWA_KV_HEADS
    ins = [sink, q]
    specs = [pl.BlockSpec(memory_space=pltpu.SMEM),
             pl.BlockSpec((1, tq, gw), lambda bb, h, i: (bb, i, h))]
    if has_win:
        ins += [kd, vd]
        specs += [pl.BlockSpec((1, l, LANES), lambda bb, h, i: (bb, 0, h))] * 2
    ins += [kcd, vcd]
    specs += [pl.BlockSpec((1, n_ctx, LANES), lambda bb, h, i: (bb, 0, h))] * 2
    return pl.pallas_call(
        functools.partial(_swa_attn_body, has_win, tq, sub, l),
        grid=(b, SWA_KV_HEADS, l // tq),
        in_specs=specs,
        out_specs=pl.BlockSpec((1, tq, gw), lambda bb, h, i: (bb, i, h)),
        out_shape=jax.ShapeDtypeStruct((b, l, dq), BF16),
        scratch_shapes=[pltpu.VMEM((2, sub, n_ctx + (sub + 2 * WINDOW if has_win else 0)), F32)],
        compiler_params=_cparams("parallel", "parallel", "parallel"),
        name="swa_attn_lat" if has_win else "swa_attn_ctx",
    )(*ins)


def _mla_proj_body(rope, x_ref, gain_ref, sh_ref, sc_ref, wd_ref, gql_ref, wuq_ref, gkvl_ref,
                   wuk_ref, wuv_ref, g128_ref, perm_ref, gkn_ref, cq_ref, ck_ref, *rest):
    if rope:
        sq_ref, sk_ref, q_out, k_out, v_out = rest
        sq, sk = sq_ref[...], sk_ref[...]
    else:
        q_out, k_out, v_out = rest
        sq = sk = None
    h = _modnorm(x_ref[0], gain_ref[...], sh_ref[0], sc_ref[0]).astype(BF16)
    t = _dot(h, wd_ref[...])
    cq = t[:, :Q_LORA]
    ckv = t[:, Q_LORA:Q_LORA + KV_LORA]
    kr = t[:, Q_LORA + KV_LORA:]
    cq = (cq * lax.rsqrt(jnp.mean(cq * cq, axis=-1, keepdims=True) + EPS) * gql_ref[...]).astype(BF16)
    ckv = (ckv * lax.rsqrt(jnp.mean(ckv * ckv, axis=-1, keepdims=True) + EPS) * gkvl_ref[...]).astype(BF16)
    qa = _dot(cq, wuq_ref[...])
    ka = _dot(ckv, wuk_ref[...])
    v_out[0] = _dot(ckv, wuv_ref[...]).astype(BF16)
    g128, perm = g128_ref[...], perm_ref[...]
    q = _norm_rope(qa, g128, perm, cq_ref[...], sq, MLA_QK * EPS).astype(BF16)
    kr_rot = kr * ck_ref[...]
    if rope:
        kr_rot = kr_rot + _dot(kr.astype(BF16), perm[:LANES, :LANES]) * sk
    width = ka.shape[1]
    kraw = ka + _lane_tile(kr, width)
    ssq = _chunk_dot((kraw * kraw).astype(BF16), g128)
    k = ((ka * _lane_tile(gkn_ref[...], width) + _lane_tile(kr_rot, width))
         * lax.rsqrt(ssq + MLA_QK * EPS)).astype(BF16)
    for hh in range(MLA_HEADS):
        q_out[0, hh] = q[:, hh * LANES:(hh + 1) * LANES]
        k_out[0, hh] = k[:, hh * LANES:(hh + 1) * LANES]


def _mla_proj(x, gain, sh, sc, w, tables, rope, tm):
    b, l, d = x.shape
    ins = [x, gain, sh, sc] + list(w) + list(tables)
    specs = ([_rows(tm, d), _const(gain.shape), _vec(d), _vec(d)] + [_const(a.shape) for a in w]
             + _table_specs(tables, rope, tm))
    hspec = pl.BlockSpec((1, MLA_HEADS, tm, LANES), lambda bb, i: (bb, 0, i, 0))
    hshape = jax.ShapeDtypeStruct((b, MLA_HEADS, l, LANES), BF16)
    nv = w[5].shape[1]
    return pl.pallas_call(
        functools.partial(_mla_proj_body, rope),
        grid=(b, l // tm),
        in_specs=specs,
        out_specs=[hspec, hspec, _rows(tm, nv)],
        out_shape=[hshape, hshape, jax.ShapeDtypeStruct((b, l, nv), BF16)],
        compiler_params=_cparams("parallel", "parallel"),
        name="mla_proj_lat" if rope else "mla_proj_ctx",
    )(*ins)


def _mla_attn_body(has_lat, tk, sub, q_ref, *refs):
    if has_lat:
        kl_ref, vl_ref, kc_ref, vc_ref, o_ref, s_scr = refs
    else:
        kc_ref, vc_ref, o_ref, s_scr = refs
    tq = q_ref.shape[2]
    n_ctx = kc_ref.shape[2]
    low = lax.broadcasted_iota(jnp.int32, (sub, LANES), 1) < MLA_NOPE
    chains = []
    for r in range(tq // sub):
        for e in range(2):
            parts = []
            if has_lat:
                for j in range(kl_ref.shape[2] // tk):
                    parts.append((lambda e=e, j=j: kl_ref[0, e, j * tk:(j + 1) * tk, :],
                                  lambda j=j: vl_ref[0, j * tk:(j + 1) * tk, :], None, tk))
            parts.append((lambda e=e: kc_ref[0, e], lambda: vc_ref[0], None, n_ctx))
            chains.append((q_ref[0, e, r * sub:(r + 1) * sub, :], parts, None))
    o = _attend(chains, s_scr)
    for r in range(tq // sub):
        o_ref[0, r * sub:(r + 1) * sub, :] = jnp.where(low, o[2 * r], o[2 * r + 1]).astype(o_ref.dtype)


def _mla_attn(q, kl, vl, kc, vc, tq, sub):
    b, nh, l, _ = q.shape
    n_ctx = kc.shape[2]
    has_lat = kl is not None
    ins = [q]
    specs = [pl.BlockSpec((1, 2, tq, LANES), lambda bb, j, i: (bb, j, i, 0))]
    if has_lat:
        ins += [kl, vl]
        specs += [pl.BlockSpec((1, 2, l, LANES), lambda bb, j, i: (bb, j, 0, 0)),
                  pl.BlockSpec((1, l, LANES), lambda bb, j, i: (bb, 0, j))]
    ins += [kc, vc]
    specs += [pl.BlockSpec((1, 2, n_ctx, LANES), lambda bb, j, i: (bb, j, 0, 0)),
              pl.BlockSpec((1, n_ctx, LANES), lambda bb, j, i: (bb, 0, j))]
    return pl.pallas_call(
        functools.partial(_mla_attn_body, has_lat, MLA_KEY_BLOCK, sub),
        grid=(b, nh // 2, l // tq),
        in_specs=specs,
        out_specs=pl.BlockSpec((1, tq, LANES), lambda bb, j, i: (bb, i, j)),
        out_shape=jax.ShapeDtypeStruct((b, l, nh * MLA_NOPE), BF16),
        scratch_shapes=[pltpu.VMEM((2, sub, n_ctx + (l if has_lat else 0)), F32)],
        compiler_params=_cparams("parallel", "parallel", "parallel"),
        name="mla_attn_lat" if has_lat else "mla_attn_ctx",
    )(*ins)


def _lru_in_body(x_ref, gain_ref, sh_ref, sc_ref, w_ref, g_out, x_out):
    h = _modnorm(x_ref[0], gain_ref[...], sh_ref[0], sc_ref[0]).astype(BF16)
    u = _dot(h, w_ref[...])
    width = u.shape[1] // 2
    g_out[0] = jax.nn.gelu(u[:, :width], approximate=True).astype(BF16)
    x_out[0] = u[:, width:]


def _lru_in(x, gain, sh, sc, w_in, tm):
    b, l, d = x.shape
    width = w_in.shape[1] // 2
    return pl.pallas_call(
        _lru_in_body,
        grid=(b, l // tm),
        in_specs=[_rows(tm, d), _const(gain.shape), _vec(d), _vec(d), _const(w_in.shape)],
        out_specs=[_rows(tm, width), _rows(tm, width)],
        out_shape=[jax.ShapeDtypeStruct((b, l, width), BF16), jax.ShapeDtypeStruct((b, l, width), F32)],
        compiler_params=_cparams("parallel", "parallel"),
        name="lru_in",
    )(x, gain, sh, sc, w_in)


def _halo_specs(tm, l, d, halo=HALO):
    nb = tm // halo
    last = l // halo - 1
    prev = pl.BlockSpec((1, halo, d), lambda b, i: (b, jnp.maximum(i * nb - 1, 0), 0))
    nxt = pl.BlockSpec((1, halo, d), lambda b, i: (b, jnp.minimum((i + 1) * nb, last), 0))
    return prev, nxt


def _lru_coef_body(x_ref, xp_ref, xn_ref, cw_ref, gw_ref, gb_ref, lam_ref, af_out, bf_out, ar_out, br_out):
    i = pl.program_id(1)
    tm = x_ref.shape[1]
    x = x_ref[0]
    pm = (i > 0).astype(F32)
    nm = (i < pl.num_programs(1) - 1).astype(F32)
    p2 = xp_ref[0, HALO - 2:HALO - 1, :] * pm
    p1 = xp_ref[0, HALO - 1:HALO, :] * pm
    n1 = xn_ref[0, 0:1, :] * nm
    row = lax.broadcasted_iota(jnp.int32, x.shape, 0)
    xm1 = jnp.where(row == 0, p1, pltpu.roll(x, 1, 0))
    xm2 = jnp.where(row == 0, p2, jnp.where(row == 1, p1, pltpu.roll(x, 2, 0)))
    xp1 = jnp.where(row == tm - 1, n1, pltpu.roll(x, tm - 1, 0))
    cw = cw_ref[...]
    xc = cw[0:1] * xm2 + cw[1:2] * xm1 + cw[2:3] * x + cw[3:4] * xp1 + cw[4:5]
    xcb = xc.astype(BF16)
    outs = ((af_out, bf_out), (ar_out, br_out))
    for d in range(2):
        gates = []
        for k in range(2):
            g = jnp.concatenate(
                [_dot(xcb[:, n * LRU_BW:(n + 1) * LRU_BW], gw_ref[(d * 2 + k) * LRU_BLOCKS + n])
                 for n in range(LRU_BLOCKS)], axis=1)
            gates.append(jax.nn.sigmoid(g + gb_ref[d * 2 + k:d * 2 + k + 1, :]))
        r, ig = gates
        nl = -lam_ref[d:d + 1, :]
        softplus = jnp.maximum(nl, 0.0) + jnp.log1p(jnp.exp(-jnp.abs(nl)))
        log_a = (-LRU_C) * r * softplus
        a = jnp.exp(log_a)
        outs[d][0][0] = a
        outs[d][1][0] = jnp.sqrt(-jnp.tanh(log_a) * (a * a + 1.0)) * (ig * xc)


def _lru_coef(xr, cw, gw, gb, lam, tm):
    b, l, d = xr.shape
    prev, nxt = _halo_specs(tm, l, d)
    out = jax.ShapeDtypeStruct((b, l, d), F32)
    return pl.pallas_call(
        _lru_coef_body,
        grid=(b, l // tm),
        in_specs=[_rows(tm, d), prev, nxt, _const(cw.shape), _const(gw.shape), _const(gb.shape),
                  _const(lam.shape)],
        out_specs=[_rows(tm, d)] * 4,
        out_shape=[out] * 4,
        compiler_params=_cparams("parallel", "parallel"),
        name="lru_coef",
    )(xr, xr, xr, cw, gw, gb, lam)


def _scan_body(reverse, has_add, a_ref, b_ref, h0_ref, *rest):
    if has_add:
        add_ref, o_ref, he_ref, h_s, a_s, b_s = rest
    else:
        o_ref, he_ref, h_s, a_s, b_s = rest
    tm, width = a_ref.shape[1], a_ref.shape[2]
    n_groups = tm // SUBLANES

    @pl.when(pl.program_id(1) == 0)
    def _():
        h_s[...] = jnp.broadcast_to(h0_ref[0], h_s.shape)

    a, b = a_ref[0], b_ref[0]
    in_group = lax.broadcasted_iota(jnp.int32, a.shape, 0) & (SUBLANES - 1)
    for k in (1, 2, 4):
        if reverse:
            a_sh, b_sh = pltpu.roll(a, tm - k, 0), pltpu.roll(b, tm - k, 0)
            valid = in_group < SUBLANES - k
        else:
            a_sh, b_sh = pltpu.roll(a, k, 0), pltpu.roll(b, k, 0)
            valid = in_group >= k
        b = jnp.where(valid, b + a * b_sh, b)
        a = jnp.where(valid, a * a_sh, a)
    a_s[...] = a
    b_s[...] = b

    def group(j, carry):
        g = n_groups - 1 - j if reverse else j
        rows = pl.ds(pl.multiple_of(g * SUBLANES, SUBLANES), SUBLANES)
        h = a_s[rows, :] * carry + b_s[rows, :]
        o_ref[0, rows, :] = h + add_ref[0, rows, :] if has_add else h
        last = h[0:1, :] if reverse else h[SUBLANES - 1:SUBLANES, :]
        return jnp.broadcast_to(last, carry.shape)

    carry = lax.fori_loop(0, n_groups, group, h_s[...], unroll=8)
    h_s[...] = carry
    he_ref[0] = carry[0:1, :]


def _scan(a, b, h0, add, reverse, tm):
    nb, t, width = a.shape
    nt = t // tm
    imap = (lambda bb, i: (bb, nt - 1 - i, 0)) if reverse else (lambda bb, i: (bb, i, 0))
    blk = pl.BlockSpec((1, tm, width), imap)
    hspec = pl.BlockSpec((1, 1, width), lambda bb, i: (bb, 0, 0))
    ins = [a, b, h0] + ([add] if add is not None else [])
    specs = [blk, blk, hspec] + ([blk] if add is not None else [])
    return pl.pallas_call(
        functools.partial(_scan_body, reverse, add is not None),
        grid=(nb, nt),
        in_specs=specs,
        out_specs=[blk, hspec],
        out_shape=[jax.ShapeDtypeStruct((nb, t, width), F32), jax.ShapeDtypeStruct((nb, 1, width), F32)],
        scratch_shapes=[pltpu.VMEM((SUBLANES, width), F32), pltpu.VMEM((tm, width), F32),
                        pltpu.VMEM((tm, width), F32)],
        compiler_params=_cparams("parallel", "arbitrary"),
        name="lru_scan_rev" if reverse else "lru_scan_fwd",
    )(*ins)


def _block_tail_body(has_r, x_ref, xp_ref, xn_ref, o_ref, op_ref, on_ref, *rest):
    if has_r:
        r_ref, rp_ref, rn_ref = rest[:3]
        rest = rest[3:]
    (wo_ref, g1_ref, gain_ref, sh_ref, sc_ref, g2_ref, wu_ref, cw_ref, wd_ref, y_ref, x1_s, h_s, act_s) = rest
    i = pl.program_id(1)
    tm = x_ref.shape[1]
    d_ff = wd_ref.shape[0]
    gain, sh, sc, g1 = gain_ref[...], sh_ref[0], sc_ref[0], g1_ref[0]

    def mixed(o, r):
        return o if r is None else (o.astype(F32) * r).astype(BF16)

    wo = wo_ref[...]
    x1 = x_ref[0] + g1 * _dot(mixed(o_ref[0], r_ref[0] if has_r else None), wo)
    x1_s[...] = x1
    h_s[0:tm, :] = _modnorm(x1, gain, sh, sc).astype(BF16)
    o_halo = jnp.concatenate([op_ref[0], on_ref[0]], axis=0)
    r_halo = jnp.concatenate([rp_ref[0], rn_ref[0]], axis=0) if has_r else None
    x_halo = jnp.concatenate([xp_ref[0], xn_ref[0]], axis=0)
    x1_halo = x_halo + g1 * _dot(mixed(o_halo, r_halo), wo)
    h_s[tm:tm + 2 * FFN_HALO, :] = _modnorm(x1_halo, gain, sh, sc).astype(BF16)
    pm = (i > 0).astype(F32)
    nm = (i < pl.num_programs(1) - 1).astype(F32)

    def up(c):
        lo = c * FF_CHUNK
        return (_dot(h_s[...], wu_ref[:, lo:lo + FF_CHUNK]),
                _dot(h_s[0:tm, :], wu_ref[:, d_ff + lo:d_ff + lo + FF_CHUNK]))

    nxt = up(0)
    for c in range(d_ff // FF_CHUNK):
        ge, val = nxt
        if (c + 1) * FF_CHUNK < d_ff:
            nxt = up(c + 1)
        lo = c * FF_CHUNK
        g = ge[0:tm]
        gp = ge[tm + FFN_HALO - 1:tm + FFN_HALO] * pm
        gn = ge[tm + FFN_HALO:tm + FFN_HALO + 1] * nm
        row = lax.broadcasted_iota(jnp.int32, g.shape, 0)
        gd = jnp.where(row == 0, gp, pltpu.roll(g, 1, 0))
        gu = jnp.where(row == tm - 1, gn, pltpu.roll(g, tm - 1, 0))
        cw = cw_ref[:, lo:lo + FF_CHUNK]
        y = cw[0:1] * gd + cw[1:2] * g + cw[2:3] * gu + cw[3:4]
        act_s[:, lo:lo + FF_CHUNK] = (y * jax.nn.sigmoid(y) * val).astype(BF16)
    y_ref[0] = x1_s[...] + g2_ref[0] * _dot(act_s[...], wd_ref[...])


def _block_tail(x, o, r, w_o, g1, gain, sh, sc, g2, w, tm):
    b, l, d = x.shape
    dk = o.shape[2]
    wu, cw, wd = w
    xs = [_rows(tm, d), *_halo_specs(tm, l, d, FFN_HALO)]
    os_ = [_rows(tm, dk), *_halo_specs(tm, l, dk, FFN_HALO)]
    ins = [x, x, x, o, o, o] + ([r, r, r] if r is not None else [])
    specs = xs + os_ + (os_ if r is not None else [])
    ins += [w_o, g1, gain, sh, sc, g2, wu, cw, wd]
    specs += [_const(w_o.shape), _vec(d), _const(gain.shape), _vec(d), _vec(d), _vec(d),
              _const(wu.shape), _const(cw.shape), _const(wd.shape)]
    return pl.pallas_call(
        functools.partial(_block_tail_body, r is not None),
        grid=(b, l // tm),
        in_specs=specs,
        out_specs=_rows(tm, d),
        out_shape=jax.ShapeDtypeStruct((b, l, d), F32),
        scratch_shapes=[pltpu.VMEM((tm, d), F32), pltpu.VMEM((tm + 2 * FFN_HALO, d), BF16),
                        pltpu.VMEM((tm, wd.shape[0]), BF16)],
        compiler_params=_cparams("parallel", "parallel"),
        name="block_tail",
    )(*ins)


def _rope_cs(n_tokens, rot_dim):
    pos = jnp.arange(n_tokens)
    row = (pos // GRID_W).astype(F32)
    col = (pos % GRID_W).astype(F32)
    n_freq = rot_dim // 4
    inv = ROPE_BASE ** (-jnp.arange(n_freq, dtype=F32) / n_freq)
    ar, ac = row[:, None] * inv, col[:, None] * inv
    cos = jnp.concatenate([jnp.cos(ar), jnp.cos(ar), jnp.cos(ac), jnp.cos(ac)], axis=1)
    sin = jnp.concatenate([-jnp.sin(ar), jnp.sin(ar), -jnp.sin(ac), jnp.sin(ac)], axis=1)
    return cos, sin


def _pad_rows(a, rows):
    return jnp.pad(a, ((0, rows - a.shape[0]), (0, 0)))


def _pair_swap(n, half):
    i = jnp.arange(n)
    return jnp.where((i & half) == 0, i + half, i - half)


def _perm_matrix(half):
    return (jnp.arange(MXU_COLS)[:, None] == _pair_swap(MXU_COLS, half)[None, :]).astype(BF16)


def _group_ones(size):
    g = jnp.arange(MXU_COLS) // size
    return (g[:, None] == g[None, :]).astype(BF16)


def _gain_tables(gain, cos, sin, half):
    return gain[None, :] * cos, gain[_pair_swap(LANES, half)][None, :] * sin, gain[None, :]


def kernel(x, c, ctx, c_ctx, norm1, norm2, mod_w, mod_b, swa_w_qkv, swa_q_gain, swa_k_gain, swa_sink, swa_w_o, mla_w_down, mla_q_lora_gain, mla_w_uq, mla_kv_lora_gain, mla_w_uk, mla_w_uv, mla_q_gain, mla_k_gain, mla_w_o, lru_w_in, lru_conv_w, lru_conv_b, lru_gate_w, lru_gate_b, lru_lam, lru_w_out, ffn_w_up, ffn_conv_w, ffn_conv_b, ffn_w_down):
    bsz, seq, d = x.shape
    n_ctx = ctx.shape[1]
    depth = norm1.shape[0]
    tm_lat, tm_ctx = 512, n_ctx

    cond = _pad_rows(jnp.concatenate([c, c_ctx[None, :]], axis=0), 16)
    mods = _mod_all(cond, mod_w, mod_b)

    cos_s, sin_s = _rope_cs(seq, HEAD_DIM)
    cos_s, sin_s = jnp.tile(cos_s, (1, 2)), jnp.tile(sin_s, (1, 2))
    cos_m, sin_m = _rope_cs(seq, MLA_ROPE)
    one = jnp.ones((seq, MLA_NOPE), F32)
    zero = jnp.zeros((seq, MLA_NOPE), F32)
    cos_m = jnp.concatenate([one, cos_m, one[:, :LANES - MLA_QK]], axis=1)
    sin_m = jnp.concatenate([zero, sin_m, zero[:, :LANES - MLA_QK]], axis=1)
    ones64, ones128 = _group_ones(HEAD_DIM), _group_ones(LANES)

    for layer in range(depth):
        kind, idx = layer % 3, layer // 3
        with_ctx_out = layer < depth - 1
        m = mods[layer].reshape(16, 6, d)
        ml = [m[:bsz, j][:, None, :] for j in range(6)]
        mc = [jnp.broadcast_to(m[bsz, j][None, None, :], (bsz, 1, d)) for j in range(6)]
        n1 = norm1[layer][None, :]
        n2 = norm2[layer][None, :]
        r_lat = r_ctx = None
        if kind == 0:
            wqkv = swa_w_qkv[idx]
            nq = SWA_HEADS * HEAD_DIM
            nkv = SWA_KV_HEADS * HEAD_DIM

            def dup(w):
                w = w.reshape(d, SWA_KV_HEADS, 1, HEAD_DIM)
                return jnp.broadcast_to(w, (d, SWA_KV_HEADS, 2, HEAD_DIM)).reshape(d, 2 * nkv).astype(BF16)

            w = (wqkv[:, :nq].astype(BF16), dup(wqkv[:, nq:nq + nkv]), dup(wqkv[:, nq + nkv:]),
                 ones64, ones128, _perm_matrix(HEAD_DIM // 4))
            cq, sq, gq = _gain_tables(jnp.tile(swa_q_gain[idx], 2) * LOG2E, cos_s, sin_s, HEAD_DIM // 4)
            ck, sk, gk = _gain_tables(jnp.tile(swa_k_gain[idx], 2) * (2 * HEAD_DIM) ** 0.5, cos_s, sin_s,
                                      HEAD_DIM // 4)
            sink = swa_sink[idx] * LOG2E
            qc, kc, vc = _swa_proj(ctx, n1, mc[0], mc[1], w, (gq, gk), False, tm_ctx)
            ql, kl, vl = _swa_proj(x, n1, ml[0], ml[1], w, (cq, ck, sq, sk), True, tm_lat)
            o_lat = _swa_attn(sink, ql, kl, vl, kc, vc, 512, 256)
            o_ctx = _swa_attn(sink, qc, None, None, kc, vc, n_ctx, n_ctx) if with_ctx_out else None
            w_o = swa_w_o[idx].astype(BF16)
        elif kind == 1:
            wdn = mla_w_down[idx]
            zc = jnp.zeros((d, MLA_NOPE), F32)
            wd_pad = jnp.concatenate([wdn[:, :Q_LORA + KV_LORA], zc, wdn[:, Q_LORA + KV_LORA:],
                                      zc[:, :LANES - MLA_QK]], axis=1).astype(BF16)
            wuq = jnp.pad(mla_w_uq[idx].reshape(Q_LORA, MLA_HEADS, MLA_QK),
                          ((0, 0), (0, 0), (0, LANES - MLA_QK))).reshape(Q_LORA, MLA_HEADS * LANES).astype(BF16)
            wuk = jnp.pad(mla_w_uk[idx].reshape(KV_LORA, MLA_HEADS, MLA_NOPE),
                          ((0, 0), (0, 0), (0, LANES - MLA_NOPE))).reshape(KV_LORA, MLA_HEADS * LANES).astype(BF16)
            root = MLA_QK ** 0.5
            gq = jnp.pad(mla_q_gain[idx] * LOG2E, (0, LANES - MLA_QK))
            gk_nope = jnp.pad(mla_k_gain[idx][:MLA_NOPE] * root, (0, LANES - MLA_NOPE))[None, :]
            gk_rope = jnp.pad(mla_k_gain[idx][MLA_NOPE:] * root, (MLA_NOPE, LANES - MLA_QK))
            cq, sq, gq = _gain_tables(gq, cos_m, sin_m, MLA_ROPE // 4)
            ck, sk, gk = _gain_tables(gk_rope, cos_m, sin_m, MLA_ROPE // 4)
            w = (wd_pad, mla_q_lora_gain[idx][None, :], wuq, mla_kv_lora_gain[idx][None, :], wuk,
                 mla_w_uv[idx].astype(BF16), ones128, _perm_matrix(MLA_ROPE // 4), gk_nope)
            qc, kc, vc = _mla_proj(ctx, n1, mc[0], mc[1], w, (gq, gk), False, tm_ctx)
            ql, kl, vl = _mla_proj(x, n1, ml[0], ml[1], w, (cq, ck, sq, sk), True, tm_lat)
            o_lat = _mla_attn(ql, kl, vl, kc, vc, 1024, 256)
            o_ctx = _mla_attn(qc, None, None, kc, vc, n_ctx, n_ctx) if with_ctx_out else None
            w_o = mla_w_o[idx].astype(BF16)
        else:
            w_in = lru_w_in[idx].astype(BF16)
            cw = _pad_rows(jnp.concatenate([lru_conv_w[idx], lru_conv_b[idx][None, :]], axis=0), 8)
            gw = lru_gate_w[idx].reshape(2 * 2 * LRU_BLOCKS, LRU_BW, LRU_BW).astype(BF16)
            gb = _pad_rows(lru_gate_b[idx].reshape(4, d), 8)
            lam = _pad_rows(lru_lam[idx], 8)
            g_ctx, xr_ctx = _lru_in(ctx, n1, mc[0], mc[1], w_in, tm_ctx)
            g_lat, xr_lat = _lru_in(x, n1, ml[0], ml[1], w_in, tm_lat)
            afc, bfc, arc, brc = _lru_coef(xr_ctx, cw, gw, gb, lam, tm_ctx)
            afl, bfl, arl, brl = _lru_coef(xr_lat, cw, gw, gb, lam, tm_lat)
            h0 = jnp.zeros((bsz, 1, d), F32)
            sfc, hfc = _scan(afc, bfc, h0, None, False, tm_ctx)
            r_ctx, hrc = _scan(arc, brc, h0, sfc, True, tm_ctx)
            sfl, _ = _scan(afl, bfl, hfc, None, False, tm_lat)
            r_lat, _ = _scan(arl, brl, hrc, sfl, True, tm_lat)
            o_lat, o_ctx = g_lat, (g_ctx if with_ctx_out else None)
            w_o = lru_w_out[idx].astype(BF16)

        cw_ffn = _pad_rows(jnp.concatenate([ffn_conv_w[layer], ffn_conv_b[layer][None, :]], axis=0), 8)
        wf = (ffn_w_up[layer].astype(BF16), cw_ffn, ffn_w_down[layer].astype(BF16))

        x = _block_tail(x, o_lat, r_lat, w_o, ml[2], n2, ml[3], ml[4], ml[5], wf, tm_lat)
        if with_ctx_out:
            ctx = _block_tail(ctx, o_ctx, r_ctx, w_o, mc[2], n2, mc[3], mc[4], mc[5], wf, tm_ctx)
    return x
```

```python
import functools

import jax
import jax.numpy as jnp
from jax import lax
from jax.experimental import pallas as pl
from jax.experimental.pallas import tpu as pltpu

F32 = jnp.float32
BF16 = jnp.bfloat16

GRID_W = 64
HEAD_DIM = 64
SWA_HEADS = 16
SWA_KV_HEADS = 4
WINDOW = 128
MLA_HEADS = 16
MLA_NOPE = 64
MLA_ROPE = 32
MLA_QK = MLA_NOPE + MLA_ROPE
Q_LORA = 384
KV_LORA = 256
LRU_BLOCKS = 4
LRU_BW = 256
LRU_C = 8.0
D_FF = 2816
ROPE_BASE = 10000.0
EPS = 1e-6
NEG_INF = -1e30
LOG2E = 1.4426950408889634

LANES = 128
MXU_COLS = 256
FF_CHUNK = MXU_COLS
SUBLANES = 8
HALO = SUBLANES
FFN_HALO = 16
MLA_KEY_BLOCK = 512
VMEM_LIMIT = 56 * 1024 * 1024


def _cparams(*sem):
    return pltpu.CompilerParams(dimension_semantics=sem, vmem_limit_bytes=VMEM_LIMIT)


def _const(shape):
    nd = len(shape)
    return pl.BlockSpec(shape, lambda *_: (0,) * nd, pipeline_mode=pl.Buffered(1))


def _rows(tm, d):
    return pl.BlockSpec((1, tm, d), lambda b, i: (b, i, 0))


def _vec(d):
    return pl.BlockSpec((1, 1, d), lambda b, i: (b, 0, 0))


def _dot(a, b):
    return jnp.dot(a, b, preferred_element_type=F32)


def _dot_t(a, b):
    return lax.dot_general(a, b, (((1,), (1,)), ((), ())), preferred_element_type=F32)


def _modnorm(x, gain, shift, scale):
    ms = jnp.mean(x * x, axis=-1, keepdims=True)
    y = x * lax.rsqrt(ms + EPS) * gain
    return y * (1.0 + scale) + shift


def _chunk_dot(x, m):
    w = m.shape[0]
    return jnp.concatenate([_dot(x[:, j * w:(j + 1) * w], m) for j in range(x.shape[1] // w)], axis=1)


def _lane_tile(t, width):
    return jnp.concatenate([t] * (width // t.shape[1]), axis=1)


def _norm_rope(x, group_ones, perm, cos, sin, eps_sum):
    ssq = _chunk_dot((x * x).astype(BF16), group_ones)
    y = x * _lane_tile(cos, x.shape[1])
    if sin is not None:
        y = y + _chunk_dot(x.astype(BF16), perm) * _lane_tile(sin, x.shape[1])
    return y * lax.rsqrt(ssq + eps_sum)


def _mod_body(c_ref, w_ref, b_ref, o_ref):
    c = c_ref[...]
    cs = (c * jax.nn.sigmoid(c)).astype(BF16)
    o_ref[0] = _dot(cs, w_ref[0].astype(BF16)) + b_ref[0]


def _mod_all(cond, mod_w, mod_b):
    depth, d, n = mod_w.shape
    tn = 1536
    rows = cond.shape[0]
    return pl.pallas_call(
        _mod_body,
        grid=(depth, n // tn),
        in_specs=[pl.BlockSpec((rows, d), lambda l, j: (0, 0)),
                  pl.BlockSpec((1, d, tn), lambda l, j: (l, 0, j)),
                  pl.BlockSpec((1, 1, tn), lambda l, j: (l, 0, j))],
        out_specs=pl.BlockSpec((1, rows, tn), lambda l, j: (l, 0, j)),
        out_shape=jax.ShapeDtypeStruct((depth, rows, n), F32),
        compiler_params=_cparams("arbitrary", "arbitrary"),
        name="mod",
    )(cond, mod_w, mod_b.reshape(depth, 1, n))


def _swa_proj_body(rope, x_ref, gain_ref, sh_ref, sc_ref, wq_ref, wk_ref, wv_ref, g64_ref, g128_ref,
                   perm_ref, cq_ref, ck_ref, *rest):
    if rope:
        sq_ref, sk_ref, q_out, k_out, v_out = rest
        sq, sk = sq_ref[...], sk_ref[...]
    else:
        q_out, k_out, v_out = rest
        sq = sk = None
    h = _modnorm(x_ref[0], gain_ref[...], sh_ref[0], sc_ref[0]).astype(BF16)
    q = _dot(h, wq_ref[...])
    k = _dot(h, wk_ref[...])
    v_out[0] = _dot(h, wv_ref[...]).astype(BF16)
    perm = perm_ref[...]
    q_out[0] = _norm_rope(q, g64_ref[...], perm, cq_ref[...], sq, HEAD_DIM * EPS).astype(BF16)
    k_out[0] = _norm_rope(k, g128_ref[...], perm, ck_ref[...], sk, 2 * HEAD_DIM * EPS).astype(BF16)


def _table_specs(tables, rope, tm):
    if rope:
        return [pl.BlockSpec((tm, LANES), lambda bb, i: (i, 0))] * len(tables)
    return [_const(t.shape) for t in tables]


def _swa_proj(x, gain, sh, sc, w, tables, rope, tm):
    b, l, d = x.shape
    wq, wk, wv = w[:3]
    ins = [x, gain, sh, sc] + list(w) + list(tables)
    specs = ([_rows(tm, d), _const(gain.shape), _vec(d), _vec(d)] + [_const(a.shape) for a in w]
             + _table_specs(tables, rope, tm))
    nq, nk = wq.shape[1], wk.shape[1]
    return pl.pallas_call(
        functools.partial(_swa_proj_body, rope),
        grid=(b, l // tm),
        in_specs=specs,
        out_specs=[_rows(tm, nq), _rows(tm, nk), _rows(tm, nk)],
        out_shape=[jax.ShapeDtypeStruct((b, l, nq), BF16), jax.ShapeDtypeStruct((b, l, nk), BF16),
                   jax.ShapeDtypeStruct((b, l, nk), BF16)],
        compiler_params=_cparams("parallel", "parallel"),
        name="swa_proj_lat" if rope else "swa_proj_ctx",
    )(*ins)


def _fold(x, op):
    acc = x[:, :LANES]
    for t in range(1, x.shape[1] // LANES):
        acc = op(acc, x[:, t * LANES:(t + 1) * LANES])
    return acc


def _attend(chains, s_scr):
    def phase1(ci):
        q, parts, _ = chains[ci]
        st = {"mx": None}

        def make(p, off):
            def step():
                k_thunk, _, bias, nk = parts[p]
                s = _dot_t(q, k_thunk())
                if bias is not None:
                    s = s + bias
                s_scr[ci % 2, :, off:off + nk] = s
                f = _fold(s, jnp.maximum)
                st["mx"] = f if st["mx"] is None else jnp.maximum(st["mx"], f)
            return step

        steps, off = [], 0
        for p, part in enumerate(parts):
            steps.append(make(p, off))
            off += part[3]
        return st, steps

    def phase2(ci, st):
        _, parts, sink = chains[ci]
        out = {"acc": [None, None]}

        def make(p, off):
            def step():
                if p == 0:
                    m = jnp.max(st["mx"], axis=-1, keepdims=True)
                    out["m"] = m if sink is None else jnp.maximum(m, sink)
                nk = parts[p][3]
                pr = jnp.exp2(s_scr[ci % 2, :, off:off + nk] - out["m"])
                v = parts[p][1]()
                pv = _dot(pr.astype(BF16), jnp.concatenate([v, jnp.ones_like(v)], axis=1))
                out["acc"][p % 2] = pv if out["acc"][p % 2] is None else out["acc"][p % 2] + pv
            return step

        def finish():
            acc = out["acc"][0] if out["acc"][1] is None else out["acc"][0] + out["acc"][1]
            den = acc[:, LANES:]
            if sink is not None:
                den = den + jnp.exp2(sink - out["m"])
            return acc[:, :LANES] / den

        steps, off = [], 0
        for p, part in enumerate(parts):
            steps.append(make(p, off))
            off += part[3]
        return steps, finish

    results = []
    pending = None
    for ci in range(len(chains) + 1):
        st, steps1 = phase1(ci) if ci < len(chains) else (None, [])
        steps2 = pending[0] if pending else []
        for k in range(max(len(steps1), len(steps2))):
            if k < len(steps1):
                steps1[k]()
            if k < len(steps2):
                steps2[k]()
        if pending:
            results.append(pending[1]())
        pending = phase2(ci, st) if ci < len(chains) else None
    return results


def _swa_attn_body(has_win, tq, sub, seq, sink_ref, q_ref, *refs):
    if has_win:
        kd_ref, vd_ref, kc_ref, vc_ref, o_ref, s_scr = refs
    else:
        kc_ref, vc_ref, o_ref, s_scr = refs
    hk = pl.program_id(1)
    n_ctx = kc_ref.shape[1]
    span = sub + 2 * WINDOW
    low = lax.broadcasted_iota(jnp.int32, (sub, LANES), 1) < HEAD_DIM
    chains = []
    for r in range(tq // sub):
        if has_win:
            s0 = pl.program_id(2) * tq + r * sub
            ws = pl.multiple_of(jnp.clip(s0 - WINDOW, 0, seq - span), LANES)
            qpos = s0 + lax.broadcasted_iota(jnp.int32, (sub, span), 0)
            kpos = ws + lax.broadcasted_iota(jnp.int32, (sub, span), 1)
            bias = jnp.where(jnp.abs(qpos - kpos) <= WINDOW, 0.0, NEG_INF).astype(F32)
        for g in range(4):
            qp = q_ref[0, r * sub:(r + 1) * sub, (g // 2) * LANES:(g // 2 + 1) * LANES]
            qm = jnp.where(low if g % 2 == 0 else jnp.logical_not(low), qp, jnp.zeros_like(qp))
            parts = []
            if has_win:
                parts.append((lambda ws=ws: kd_ref[0, pl.ds(ws, span), :],
                              lambda ws=ws: vd_ref[0, pl.ds(ws, span), :], bias, span))
            parts.append((lambda: kc_ref[0], lambda: vc_ref[0], None, n_ctx))
            chains.append((qm, parts, sink_ref[hk * 4 + g]))
    o = _attend(chains, s_scr)
    for r in range(tq // sub):
        o_ref[0, r * sub:(r + 1) * sub, :] = jnp.concatenate(
            [jnp.where(low, o[4 * r], o[4 * r + 1]), jnp.where(low, o[4 * r + 2], o[4 * r + 3])],
            axis=1).astype(o_ref.dtype)


def _swa_attn(sink, q, kd, vd, kcd, vcd, tq, sub):
    b, l, dq = q.shape
    n_ctx = kcd.shape[1]
    has_win = kd is not None
    gw = dq // SWA_KV_HEADS
    ins = [sink, q]
    specs = [pl.BlockSpec(memory_space=pltpu.SMEM),
             pl.BlockSpec((1, tq, gw), lambda bb, h, i: (bb, i, h))]
    if has_win:
        ins += [kd, vd]
        specs += [pl.BlockSpec((1, l, LANES), lambda bb, h, i: (bb, 0, h))] * 2
    ins += [kcd, vcd]
    specs += [pl.BlockSpec((1, n_ctx, LANES), lambda bb, h, i: (bb, 0, h))] * 2
    return pl.pallas_call(
        functools.partial(_swa_attn_body, has_win, tq, sub, l),
        grid=(b, SWA_KV_HEADS, l // tq),
        in_specs=specs,
        out_specs=pl.BlockSpec((1, tq, gw), lambda bb, h, i: (bb, i, h)),
        out_shape=jax.ShapeDtypeStruct((b, l, dq), BF16),
        scratch_shapes=[pltpu.VMEM((2, sub, n_ctx + (sub + 2 * WINDOW if has_win else 0)), F32)],
        compiler_params=_cparams("parallel", "parallel", "parallel"),
        name="swa_attn_lat" if has_win else "swa_attn_ctx",
    )(*ins)


def _mla_proj_body(rope, x_ref, gain_ref, sh_ref, sc_ref, wd_ref, gql_ref, wuq_ref, gkvl_ref,
                   wuk_ref, wuv_ref, g128_ref, perm_ref, gkn_ref, cq_ref, ck_ref, *rest):
    if rope:
        sq_ref, sk_ref, q_out, k_out, v_out = rest
        sq, sk = sq_ref[...], sk_ref[...]
    else:
        q_out, k_out, v_out = rest
        sq = sk = None
    h = _modnorm(x_ref[0], gain_ref[...], sh_ref[0], sc_ref[0]).astype(BF16)
    t = _dot(h, wd_ref[...])
    cq = t[:, :Q_LORA]
    ckv = t[:, Q_LORA:Q_LORA + KV_LORA]
    kr = t[:, Q_LORA + KV_LORA:]
    cq = (cq * lax.rsqrt(jnp.mean(cq * cq, axis=-1, keepdims=True) + EPS) * gql_ref[...]).astype(BF16)
    ckv = (ckv * lax.rsqrt(jnp.mean(ckv * ckv, axis=-1, keepdims=True) + EPS) * gkvl_ref[...]).astype(BF16)
    qa = _dot(cq, wuq_ref[...])
    ka = _dot(ckv, wuk_ref[...])
    v_out[0] = _dot(ckv, wuv_ref[...]).astype(BF16)
    g128, perm = g128_ref[...], perm_ref[...]
    q = _norm_rope(qa, g128, perm, cq_ref[...], sq, MLA_QK * EPS).astype(BF16)
    kr_rot = kr * ck_ref[...]
    if rope:
        kr_rot = kr_rot + _dot(kr.astype(BF16), perm[:LANES, :LANES]) * sk
    width = ka.shape[1]
    kraw = ka + _lane_tile(kr, width)
    ssq = _chunk_dot((kraw * kraw).astype(BF16), g128)
    k = ((ka * _lane_tile(gkn_ref[...], width) + _lane_tile(kr_rot, width))
         * lax.rsqrt(ssq + MLA_QK * EPS)).astype(BF16)
    for hh in range(MLA_HEADS):
        q_out[0, hh] = q[:, hh * LANES:(hh + 1) * LANES]
        k_out[0, hh] = k[:, hh * LANES:(hh + 1) * LANES]


def _mla_proj(x, gain, sh, sc, w, tables, rope, tm):
    b, l, d = x.shape
    ins = [x, gain, sh, sc] + list(w) + list(tables)
    specs = ([_rows(tm, d), _const(gain.shape), _vec(d), _vec(d)] + [_const(a.shape) for a in w]
             + _table_specs(tables, rope, tm))
    hspec = pl.BlockSpec((1, MLA_HEADS, tm, LANES), lambda bb, i: (bb, 0, i, 0))
    hshape = jax.ShapeDtypeStruct((b, MLA_HEADS, l, LANES), BF16)
    nv = w[5].shape[1]
    return pl.pallas_call(
        functools.partial(_mla_proj_body, rope),
        grid=(b, l // tm),
        in_specs=specs,
        out_specs=[hspec, hspec, _rows(tm, nv)],
        out_shape=[hshape, hshape, jax.ShapeDtypeStruct((b, l, nv), BF16)],
        compiler_params=_cparams("parallel", "parallel"),
        name="mla_proj_lat" if rope else "mla_proj_ctx",
    )(*ins)


def _mla_attn_body(has_lat, tk, sub, q_ref, *refs):
    if has_lat:
        kl_ref, vl_ref, kc_ref, vc_ref, o_ref, s_scr = refs
    else:
        kc_ref, vc_ref, o_ref, s_scr = refs
    tq = q_ref.shape[2]
    n_ctx = kc_ref.shape[2]
    low = lax.broadcasted_iota(jnp.int32, (sub, LANES), 1) < MLA_NOPE
    chains = []
    for r in range(tq // sub):
        for e in range(2):
            parts = []
            if has_lat:
                for j in range(kl_ref.shape[2] // tk):
                    parts.append((lambda e=e, j=j: kl_ref[0, e, j * tk:(j + 1) * tk, :],
                                  lambda j=j: vl_ref[0, j * tk:(j + 1) * tk, :], None, tk))
            parts.append((lambda e=e: kc_ref[0, e], lambda: vc_ref[0], None, n_ctx))
            chains.append((q_ref[0, e, r * sub:(r + 1) * sub, :], parts, None))
    o = _attend(chains, s_scr)
    for r in range(tq // sub):
        o_ref[0, r * sub:(r + 1) * sub, :] = jnp.where(low, o[2 * r], o[2 * r + 1]).astype(o_ref.dtype)


def _mla_attn(q, kl, vl, kc, vc, tq, sub):
    b, nh, l, _ = q.shape
    n_ctx = kc.shape[2]
    has_lat = kl is not None
    ins = [q]
    specs = [pl.BlockSpec((1, 2, tq, LANES), lambda bb, j, i: (bb, j, i, 0))]
    if has_lat:
        ins += [kl, vl]
        specs += [pl.BlockSpec((1, 2, l, LANES), lambda bb, j, i: (bb, j, 0, 0)),
                  pl.BlockSpec((1, l, LANES), lambda bb, j, i: (bb, 0, j))]
    ins += [kc, vc]
    specs += [pl.BlockSpec((1, 2, n_ctx, LANES), lambda bb, j, i: (bb, j, 0, 0)),
              pl.BlockSpec((1, n_ctx, LANES), lambda bb, j, i: (bb, 0, j))]
    return pl.pallas_call(
        functools.partial(_mla_attn_body, has_lat, MLA_KEY_BLOCK, sub),
        grid=(b, nh // 2, l // tq),
        in_specs=specs,
        out_specs=pl.BlockSpec((1, tq, LANES), lambda bb, j, i: (bb, i, j)),
        out_shape=jax.ShapeDtypeStruct((b, l, nh * MLA_NOPE), BF16),
        scratch_shapes=[pltpu.VMEM((2, sub, n_ctx + (l if has_lat else 0)), F32)],
        compiler_params=_cparams("parallel", "parallel", "parallel"),
        name="mla_attn_lat" if has_lat else "mla_attn_ctx",
    )(*ins)


def _lru_in_body(x_ref, gain_ref, sh_ref, sc_ref, w_ref, g_out, x_out):
    h = _modnorm(x_ref[0], gain_ref[...], sh_ref[0], sc_ref[0]).astype(BF16)
    u = _dot(h, w_ref[...])
    width = u.shape[1] // 2
    g_out[0] = jax.nn.gelu(u[:, :width], approximate=True).astype(BF16)
    x_out[0] = u[:, width:]


def _lru_in(x, gain, sh, sc, w_in, tm):
    b, l, d = x.shape
    width = w_in.shape[1] // 2
    return pl.pallas_call(
        _lru_in_body,
        grid=(b, l // tm),
        in_specs=[_rows(tm, d), _const(gain.shape), _vec(d), _vec(d), _const(w_in.shape)],
        out_specs=[_rows(tm, width), _rows(tm, width)],
        out_shape=[jax.ShapeDtypeStruct((b, l, width), BF16), jax.ShapeDtypeStruct((b, l, width), F32)],
        compiler_params=_cparams("parallel", "parallel"),
        name="lru_in",
    )(x, gain, sh, sc, w_in)


def _halo_specs(tm, l, d, halo=HALO):
    nb = tm // halo
    last = l // halo - 1
    prev = pl.BlockSpec((1, halo, d), lambda b, i: (b, jnp.maximum(i * nb - 1, 0), 0))
    nxt = pl.BlockSpec((1, halo, d), lambda b, i: (b, jnp.minimum((i + 1) * nb, last), 0))
    return prev, nxt


def _lru_coef_body(x_ref, xp_ref, xn_ref, cw_ref, gw_ref, gb_ref, lam_ref, af_out, bf_out, ar_out, br_out):
    i = pl.program_id(1)
    tm = x_ref.shape[1]
    x = x_ref[0]
    pm = (i > 0).astype(F32)
    nm = (i < pl.num_programs(1) - 1).astype(F32)
    p2 = xp_ref[0, HALO - 2:HALO - 1, :] * pm
    p1 = xp_ref[0, HALO - 1:HALO, :] * pm
    n1 = xn_ref[0, 0:1, :] * nm
    row = lax.broadcasted_iota(jnp.int32, x.shape, 0)
    xm1 = jnp.where(row == 0, p1, pltpu.roll(x, 1, 0))
    xm2 = jnp.where(row == 0, p2, jnp.where(row == 1, p1, pltpu.roll(x, 2, 0)))
    xp1 = jnp.where(row == tm - 1, n1, pltpu.roll(x, tm - 1, 0))
    cw = cw_ref[...]
    xc = cw[0:1] * xm2 + cw[1:2] * xm1 + cw[2:3] * x + cw[3:4] * xp1 + cw[4:5]
    xcb = xc.astype(BF16)
    outs = ((af_out, bf_out), (ar_out, br_out))
    for d in range(2):
        gates = []
        for k in range(2):
            g = jnp.concatenate(
                [_dot(xcb[:, n * LRU_BW:(n + 1) * LRU_BW], gw_ref[(d * 2 + k) * LRU_BLOCKS + n])
                 for n in range(LRU_BLOCKS)], axis=1)
            gates.append(jax.nn.sigmoid(g + gb_ref[d * 2 + k:d * 2 + k + 1, :]))
        r, ig = gates
        nl = -lam_ref[d:d + 1, :]
        softplus = jnp.maximum(nl, 0.0) + jnp.log1p(jnp.exp(-jnp.abs(nl)))
        log_a = (-LRU_C) * r * softplus
        a = jnp.exp(log_a)
        outs[d][0][0] = a
        outs[d][1][0] = jnp.sqrt(-jnp.tanh(log_a) * (a * a + 1.0)) * (ig * xc)


def _lru_coef(xr, cw, gw, gb, lam, tm):
    b, l, d = xr.shape
    prev, nxt = _halo_specs(tm, l, d)
    out = jax.ShapeDtypeStruct((b, l, d), F32)
    return pl.pallas_call(
        _lru_coef_body,
        grid=(b, l // tm),
        in_specs=[_rows(tm, d), prev, nxt, _const(cw.shape), _const(gw.shape), _const(gb.shape),
                  _const(lam.shape)],
        out_specs=[_rows(tm, d)] * 4,
        out_shape=[out] * 4,
        compiler_params=_cparams("parallel", "parallel"),
        name="lru_coef",
    )(xr, xr, xr, cw, gw, gb, lam)


def _scan_body(reverse, has_add, a_ref, b_ref, h0_ref, *rest):
    if has_add:
        add_ref, o_ref, he_ref, h_s, a_s, b_s = rest
    else:
        o_ref, he_ref, h_s, a_s, b_s = rest
    tm, width = a_ref.shape[1], a_ref.shape[2]
    n_groups = tm // SUBLANES

    @pl.when(pl.program_id(1) == 0)
    def _():
        h_s[...] = jnp.broadcast_to(h0_ref[0], h_s.shape)

    a, b = a_ref[0], b_ref[0]
    in_group = lax.broadcasted_iota(jnp.int32, a.shape, 0) & (SUBLANES - 1)
    for k in (1, 2, 4):
        if reverse:
            a_sh, b_sh = pltpu.roll(a, tm - k, 0), pltpu.roll(b, tm - k, 0)
            valid = in_group < SUBLANES - k
        else:
            a_sh, b_sh = pltpu.roll(a, k, 0), pltpu.roll(b, k, 0)
            valid = in_group >= k
        b = jnp.where(valid, b + a * b_sh, b)
        a = jnp.where(valid, a * a_sh, a)
    a_s[...] = a
    b_s[...] = b

    def group(j, carry):
        g = n_groups - 1 - j if reverse else j
        rows = pl.ds(pl.multiple_of(g * SUBLANES, SUBLANES), SUBLANES)
        h = a_s[rows, :] * carry + b_s[rows, :]
        o_ref[0, rows, :] = h + add_ref[0, rows, :] if has_add else h
        last = h[0:1, :] if reverse else h[SUBLANES - 1:SUBLANES, :]
        return jnp.broadcast_to(last, carry.shape)

    carry = lax.fori_loop(0, n_groups, group, h_s[...], unroll=8)
    h_s[...] = carry
    he_ref[0] = carry[0:1, :]


def _scan(a, b, h0, add, reverse, tm):
    nb, t, width = a.shape
    nt = t // tm
    imap = (lambda bb, i: (bb, nt - 1 - i, 0)) if reverse else (lambda bb, i: (bb, i, 0))
    blk = pl.BlockSpec((1, tm, width), imap)
    hspec = pl.BlockSpec((1, 1, width), lambda bb, i: (bb, 0, 0))
    ins = [a, b, h0] + ([add] if add is not None else [])
    specs = [blk, blk, hspec] + ([blk] if add is not None else [])
    return pl.pallas_call(
        functools.partial(_scan_body, reverse, add is not None),
        grid=(nb, nt),
        in_specs=specs,
        out_specs=[blk, hspec],
        out_shape=[jax.ShapeDtypeStruct((nb, t, width), F32), jax.ShapeDtypeStruct((nb, 1, width), F32)],
        scratch_shapes=[pltpu.VMEM((SUBLANES, width), F32), pltpu.VMEM((tm, width), F32),
                        pltpu.VMEM((tm, width), F32)],
        compiler_params=_cparams("parallel", "arbitrary"),
        name="lru_scan_rev" if reverse else "lru_scan_fwd",
    )(*ins)


def _block_tail_body(has_r, x_ref, xp_ref, xn_ref, o_ref, op_ref, on_ref, *rest):
    if has_r:
        r_ref, rp_ref, rn_ref = rest[:3]
        rest = rest[3:]
    (wo_ref, g1_ref, gain_ref, sh_ref, sc_ref, g2_ref, wu_ref, cw_ref, wd_ref, y_ref, x1_s, h_s, act_s) = rest
    i = pl.program_id(1)
    tm = x_ref.shape[1]
    d_ff = wd_ref.shape[0]
    gain, sh, sc, g1 = gain_ref[...], sh_ref[0], sc_ref[0], g1_ref[0]

    def mixed(o, r):
        return o if r is None else (o.astype(F32) * r).astype(BF16)

    wo = wo_ref[...]
    x1 = x_ref[0] + g1 * _dot(mixed(o_ref[0], r_ref[0] if has_r else None), wo)
    x1_s[...] = x1
    h_s[0:tm, :] = _modnorm(x1, gain, sh, sc).astype(BF16)
    o_halo = jnp.concatenate([op_ref[0], on_ref[0]], axis=0)
    r_halo = jnp.concatenate([rp_ref[0], rn_ref[0]], axis=0) if has_r else None
    x_halo = jnp.concatenate([xp_ref[0], xn_ref[0]], axis=0)
    x1_halo = x_halo + g1 * _dot(mixed(o_halo, r_halo), wo)
    h_s[tm:tm + 2 * FFN_HALO, :] = _modnorm(x1_halo, gain, sh, sc).astype(BF16)
    pm = (i > 0).astype(F32)
    nm = (i < pl.num_programs(1) - 1).astype(F32)

    def up(c):
        lo = c * FF_CHUNK
        return (_dot(h_s[...], wu_ref[:, lo:lo + FF_CHUNK]),
                _dot(h_s[0:tm, :], wu_ref[:, d_ff + lo:d_ff + lo + FF_CHUNK]))

    nxt = up(0)
    for c in range(d_ff // FF_CHUNK):
        ge, val = nxt
        if (c + 1) * FF_CHUNK < d_ff:
            nxt = up(c + 1)
        lo = c * FF_CHUNK
        g = ge[0:tm]
        gp = ge[tm + FFN_HALO - 1:tm + FFN_HALO] * pm
        gn = ge[tm + FFN_HALO:tm + FFN_HALO + 1] * nm
        row = lax.broadcasted_iota(jnp.int32, g.shape, 0)
        gd = jnp.where(row == 0, gp, pltpu.roll(g, 1, 0))
        gu = jnp.where(row == tm - 1, gn, pltpu.roll(g, tm - 1, 0))
        cw = cw_ref[:, lo:lo + FF_CHUNK]
        y = cw[0:1] * gd + cw[1:2] * g + cw[2:3] * gu + cw[3:4]
        act_s[:, lo:lo + FF_CHUNK] = (y * jax.nn.sigmoid(y) * val).astype(BF16)
    y_ref[0] = x1_s[...] + g2_ref[0] * _dot(act_s[...], wd_ref[...])


def _block_tail(x, o, r, w_o, g1, gain, sh, sc, g2, w, tm):
    b, l, d = x.shape
    dk = o.shape[2]
    wu, cw, wd = w
    xs = [_rows(tm, d), *_halo_specs(tm, l, d, FFN_HALO)]
    os_ = [_rows(tm, dk), *_halo_specs(tm, l, dk, FFN_HALO)]
    ins = [x, x, x, o, o, o] + ([r, r, r] if r is not None else [])
    specs = xs + os_ + (os_ if r is not None else [])
    ins += [w_o, g1, gain, sh, sc, g2, wu, cw, wd]
    specs += [_const(w_o.shape), _vec(d), _const(gain.shape), _vec(d), _vec(d), _vec(d),
              _const(wu.shape), _const(cw.shape), _const(wd.shape)]
    return pl.pallas_call(
        functools.partial(_block_tail_body, r is not None),
        grid=(b, l // tm),
        in_specs=specs,
        out_specs=_rows(tm, d),
        out_shape=jax.ShapeDtypeStruct((b, l, d), F32),
        scratch_shapes=[pltpu.VMEM((tm, d), F32), pltpu.VMEM((tm + 2 * FFN_HALO, d), BF16),
                        pltpu.VMEM((tm, wd.shape[0]), BF16)],
        compiler_params=_cparams("parallel", "parallel"),
        name="block_tail",
    )(*ins)


def _rope_cs(n_tokens, rot_dim):
    pos = jnp.arange(n_tokens)
    row = (pos // GRID_W).astype(F32)
    col = (pos % GRID_W).astype(F32)
    n_freq = rot_dim // 4
    inv = ROPE_BASE ** (-jnp.arange(n_freq, dtype=F32) / n_freq)
    ar, ac = row[:, None] * inv, col[:, None] * inv
    cos = jnp.concatenate([jnp.cos(ar), jnp.cos(ar), jnp.cos(ac), jnp.cos(ac)], axis=1)
    sin = jnp.concatenate([-jnp.sin(ar), jnp.sin(ar), -jnp.sin(ac), jnp.sin(ac)], axis=1)
    return cos, sin


def _pad_rows(a, rows):
    return jnp.pad(a, ((0, rows - a.shape[0]), (0, 0)))


def _pair_swap(n, half):
    i = jnp.arange(n)
    return jnp.where((i & half) == 0, i + half, i - half)


def _perm_matrix(half):
    return (jnp.arange(MXU_COLS)[:, None] == _pair_swap(MXU_COLS, half)[None, :]).astype(BF16)


def _group_ones(size):
    g = jnp.arange(MXU_COLS) // size
    return (g[:, None] == g[None, :]).astype(BF16)


def _gain_tables(gain, cos, sin, half):
    return gain[None, :] * cos, gain[_pair_swap(LANES, half)][None, :] * sin, gain[None, :]


def kernel(x, c, ctx, c_ctx, norm1, norm2, mod_w, mod_b, swa_w_qkv, swa_q_gain, swa_k_gain, swa_sink, swa_w_o, mla_w_down, mla_q_lora_gain, mla_w_uq, mla_kv_lora_gain, mla_w_uk, mla_w_uv, mla_q_gain, mla_k_gain, mla_w_o, lru_w_in, lru_conv_w, lru_conv_b, lru_gate_w, lru_gate_b, lru_lam, lru_w_out, ffn_w_up, ffn_conv_w, ffn_conv_b, ffn_w_down):
    bsz, seq, d = x.shape
    n_ctx = ctx.shape[1]
    depth = norm1.shape[0]
    tm_lat, tm_ctx = 512, n_ctx

    cond = _pad_rows(jnp.concatenate([c, c_ctx[None, :]], axis=0), 16)
    mods = _mod_all(cond, mod_w, mod_b)

    cos_s, sin_s = _rope_cs(seq, HEAD_DIM)
    cos_s, sin_s = jnp.tile(cos_s, (1, 2)), jnp.tile(sin_s, (1, 2))
    cos_m, sin_m = _rope_cs(seq, MLA_ROPE)
    one = jnp.ones((seq, MLA_NOPE), F32)
    zero = jnp.zeros((seq, MLA_NOPE), F32)
    cos_m = jnp.concatenate([one, cos_m, one[:, :LANES - MLA_QK]], axis=1)
    sin_m = jnp.concatenate([zero, sin_m, zero[:, :LANES - MLA_QK]], axis=1)
    ones64, ones128 = _group_ones(HEAD_DIM), _group_ones(LANES)

    for layer in range(depth):
        kind, idx = layer % 3, layer // 3
        with_ctx_out = layer < depth - 1
        m = mods[layer].reshape(16, 6, d)
        ml = [m[:bsz, j][:, None, :] for j in range(6)]
        mc = [jnp.broadcast_to(m[bsz, j][None, None, :], (bsz, 1, d)) for j in range(6)]
        n1 = norm1[layer][None, :]
        n2 = norm2[layer][None, :]
        r_lat = r_ctx = None
        if kind == 0:
            wqkv = swa_w_qkv[idx]
            nq = SWA_HEADS * HEAD_DIM
            nkv = SWA_KV_HEADS * HEAD_DIM

            def dup(w):
                w = w.reshape(d, SWA_KV_HEADS, 1, HEAD_DIM)
                return jnp.broadcast_to(w, (d, SWA_KV_HEADS, 2, HEAD_DIM)).reshape(d, 2 * nkv).astype(BF16)

            w = (wqkv[:, :nq].astype(BF16), dup(wqkv[:, nq:nq + nkv]), dup(wqkv[:, nq + nkv:]),
                 ones64, ones128, _perm_matrix(HEAD_DIM // 4))
            cq, sq, gq = _gain_tables(jnp.tile(swa_q_gain[idx], 2) * LOG2E, cos_s, sin_s, HEAD_DIM // 4)
            ck, sk, gk = _gain_tables(jnp.tile(swa_k_gain[idx], 2) * (2 * HEAD_DIM) ** 0.5, cos_s, sin_s,
                                      HEAD_DIM // 4)
            sink = swa_sink[idx] * LOG2E
            qc, kc, vc = _swa_proj(ctx, n1, mc[0], mc[1], w, (gq, gk), False, tm_ctx)
            ql, kl, vl = _swa_proj(x, n1, ml[0], ml[1], w, (cq, ck, sq, sk), True, tm_lat)
            o_lat = _swa_attn(sink, ql, kl, vl, kc, vc, 512, 256)
            o_ctx = _swa_attn(sink, qc, None, None, kc, vc, n_ctx, n_ctx) if with_ctx_out else None
            w_o = swa_w_o[idx].astype(BF16)
        elif kind == 1:
            wdn = mla_w_down[idx]
            zc = jnp.zeros((d, MLA_NOPE), F32)
            wd_pad = jnp.concatenate([wdn[:, :Q_LORA + KV_LORA], zc, wdn[:, Q_LORA + KV_LORA:],
                                      zc[:, :LANES - MLA_QK]], axis=1).astype(BF16)
            wuq = jnp.pad(mla_w_uq[idx].reshape(Q_LORA, MLA_HEADS, MLA_QK),
                          ((0, 0), (0, 0), (0, LANES - MLA_QK))).reshape(Q_LORA, MLA_HEADS * LANES).astype(BF16)
            wuk = jnp.pad(mla_w_uk[idx].reshape(KV_LORA, MLA_HEADS, MLA_NOPE),
                          ((0, 0), (0, 0), (0, LANES - MLA_NOPE))).reshape(KV_LORA, MLA_HEADS * LANES).astype(BF16)
            root = MLA_QK ** 0.5
            gq = jnp.pad(mla_q_gain[idx] * LOG2E, (0, LANES - MLA_QK))
            gk_nope = jnp.pad(mla_k_gain[idx][:MLA_NOPE] * root, (0, LANES - MLA_NOPE))[None, :]
            gk_rope = jnp.pad(mla_k_gain[idx][MLA_NOPE:] * root, (MLA_NOPE, LANES - MLA_QK))
            cq, sq, gq = _gain_tables(gq, cos_m, sin_m, MLA_ROPE // 4)
            ck, sk, gk = _gain_tables(gk_rope, cos_m, sin_m, MLA_ROPE // 4)
            w = (wd_pad, mla_q_lora_gain[idx][None, :], wuq, mla_kv_lora_gain[idx][None, :], wuk,
                 mla_w_uv[idx].astype(BF16), ones128, _perm_matrix(MLA_ROPE // 4), gk_nope)
            qc, kc, vc = _mla_proj(ctx, n1, mc[0], mc[1], w, (gq, gk), False, tm_ctx)
            ql, kl, vl = _mla_proj(x, n1, ml[0], ml[1], w, (cq, ck, sq, sk), True, tm_lat)
            o_lat = _mla_attn(ql, kl, vl, kc, vc, 1024, 256)
            o_ctx = _mla_attn(qc, None, None, kc, vc, n_ctx, n_ctx) if with_ctx_out else None
            w_o = mla_w_o[idx].astype(BF16)
        else:
            w_in = lru_w_in[idx].astype(BF16)
            cw = _pad_rows(jnp.concatenate([lru_conv_w[idx], lru_conv_b[idx][None, :]], axis=0), 8)
            gw = lru_gate_w[idx].reshape(2 * 2 * LRU_BLOCKS, LRU_BW, LRU_BW).astype(BF16)
            gb = _pad_rows(lru_gate_b[idx].reshape(4, d), 8)
            lam = _pad_rows(lru_lam[idx], 8)
            g_ctx, xr_ctx = _lru_in(ctx, n1, mc[0], mc[1], w_in, tm_ctx)
            g_lat, xr_lat = _lru_in(x, n1, ml[0], ml[1], w_in, tm_lat)
            afc, bfc, arc, brc = _lru_coef(xr_ctx, cw, gw, gb, lam, tm_ctx)
            afl, bfl, arl, brl = _lru_coef(xr_lat, cw, gw, gb, lam, tm_lat)
            h0 = jnp.zeros((bsz, 1, d), F32)
            sfc, hfc = _scan(afc, bfc, h0, None, False, tm_ctx)
            r_ctx, hrc = _scan(arc, brc, h0, sfc, True, tm_ctx)
            sfl, _ = _scan(afl, bfl, hfc, None, False, tm_lat)
            r_lat, _ = _scan(arl, brl, hrc, sfl, True, tm_lat)
            o_lat, o_ctx = g_lat, (g_ctx if with_ctx_out else None)
            w_o = lru_w_out[idx].astype(BF16)

        cw_ffn = _pad_rows(jnp.concatenate([ffn_conv_w[layer], ffn_conv_b[layer][None, :]], axis=0), 8)
        wf = (ffn_w_up[layer].astype(BF16), cw_ffn, ffn_w_down[layer].astype(BF16))

        x = _block_tail(x, o_lat, r_lat, w_o, ml[2], n2, ml[3], ml[4], ml[5], wf, tm_lat)
        if with_ctx_out:
            ctx = _block_tail(ctx, o_ctx, r_ctx, w_o, mc[2], n2, mc[3], mc[4], mc[5], wf, tm_ctx)
    return x
```

```python
import functools

import jax
import jax.numpy as jnp
from jax import lax
from jax.experimental import pallas as pl
from jax.experimental.pallas import tpu as pltpu

F32 = jnp.float32
BF16 = jnp.bfloat16

GRID_W = 64
HEAD_DIM = 64
SWA_HEADS = 16
SWA_KV_HEADS = 4
WINDOW = 128
MLA_HEADS = 16
MLA_NOPE = 64
MLA_ROPE = 32
MLA_QK = MLA_NOPE + MLA_ROPE
Q_LORA = 384
KV_LORA = 256
LRU_BLOCKS = 4
LRU_BW = 256
LRU_C = 8.0
D_FF = 2816
ROPE_BASE = 10000.0
EPS = 1e-6
NEG_INF = -1e30
LOG2E = 1.4426950408889634

LANES = 128
MXU_COLS = 256
FF_CHUNK = MXU_COLS
SUBLANES = 8
HALO = SUBLANES
FFN_HALO = 16
MLA_KEY_BLOCK = 512
VMEM_LIMIT = 56 * 1024 * 1024


def _cparams(*sem):
    return pltpu.CompilerParams(dimension_semantics=sem, vmem_limit_bytes=VMEM_LIMIT)


def _const(shape):
    nd = len(shape)
    return pl.BlockSpec(shape, lambda *_: (0,) * nd, pipeline_mode=pl.Buffered(1))


def _rows(tm, d):
    return pl.BlockSpec((1, tm, d), lambda b, i: (b, i, 0))


def _vec(d):
    return pl.BlockSpec((1, 1, d), lambda b, i: (b, 0, 0))


def _dot(a, b):
    return jnp.dot(a, b, preferred_element_type=F32)


def _dot_t(a, b):
    return lax.dot_general(a, b, (((1,), (1,)), ((), ())), preferred_element_type=F32)


def _modnorm(x, gain, shift, scale):
    ms = jnp.mean(x * x, axis=-1, keepdims=True)
    y = x * lax.rsqrt(ms + EPS) * gain
    return y * (1.0 + scale) + shift


def _chunk_dot(x, m):
    w = m.shape[0]
    return jnp.concatenate([_dot(x[:, j * w:(j + 1) * w], m) for j in range(x.shape[1] // w)], axis=1)


def _lane_tile(t, width):
    return jnp.concatenate([t] * (width // t.shape[1]), axis=1)


def _norm_rope(x, group_ones, perm, cos, sin, eps_sum):
    ssq = _chunk_dot((x * x).astype(BF16), group_ones)
    y = x * _lane_tile(cos, x.shape[1])
    if sin is not None:
        y = y + _chunk_dot(x.astype(BF16), perm) * _lane_tile(sin, x.shape[1])
    return y * lax.rsqrt(ssq + eps_sum)


def _mod_body(c_ref, w_ref, b_ref, o_ref):
    c = c_ref[...]
    cs = (c * jax.nn.sigmoid(c)).astype(BF16)
    o_ref[0] = _dot(cs, w_ref[0].astype(BF16)) + b_ref[0]


def _mod_all(cond, mod_w, mod_b):
    depth, d, n = mod_w.shape
    tn = 1536
    rows = cond.shape[0]
    return pl.pallas_call(
        _mod_body,
        grid=(depth, n // tn),
        in_specs=[pl.BlockSpec((rows, d), lambda l, j: (0, 0)),
                  pl.BlockSpec((1, d, tn), lambda l, j: (l, 0, j)),
                  pl.BlockSpec((1, 1, tn), lambda l, j: (l, 0, j))],
        out_specs=pl.BlockSpec((1, rows, tn), lambda l, j: (l, 0, j)),
        out_shape=jax.ShapeDtypeStruct((depth, rows, n), F32),
        compiler_params=_cparams("arbitrary", "arbitrary"),
        name="mod",
    )(cond, mod_w, mod_b.reshape(depth, 1, n))


def _swa_proj_body(rope, x_ref, gain_ref, sh_ref, sc_ref, wq_ref, wk_ref, wv_ref, g64_ref, g128_ref,
                   perm_ref, cq_ref, ck_ref, *rest):
    if rope:
        sq_ref, sk_ref, q_out, k_out, v_out = rest
        sq, sk = sq_ref[...], sk_ref[...]
    else:
        q_out, k_out, v_out = rest
        sq = sk = None
    h = _modnorm(x_ref[0], gain_ref[...], sh_ref[0], sc_ref[0]).astype(BF16)
    q = _dot(h, wq_ref[...])
    k = _dot(h, wk_ref[...])
    v_out[0] = _dot(h, wv_ref[...]).astype(BF16)
    perm = perm_ref[...]
    q_out[0] = _norm_rope(q, g64_ref[...], perm, cq_ref[...], sq, HEAD_DIM * EPS).astype(BF16)
    k_out[0] = _norm_rope(k, g128_ref[...], perm, ck_ref[...], sk, 2 * HEAD_DIM * EPS).astype(BF16)


def _table_specs(tables, rope, tm):
    if rope:
        return [pl.BlockSpec((tm, LANES), lambda bb, i: (i, 0))] * len(tables)
    return [_const(t.shape) for t in tables]


def _swa_proj(x, gain, sh, sc, w, tables, rope, tm):
    b, l, d = x.shape
    wq, wk, wv = w[:3]
    ins = [x, gain, sh, sc] + list(w) + list(tables)
    specs = ([_rows(tm, d), _const(gain.shape), _vec(d), _vec(d)] + [_const(a.shape) for a in w]
             + _table_specs(tables, rope, tm))
    nq, nk = wq.shape[1], wk.shape[1]
    return pl.pallas_call(
        functools.partial(_swa_proj_body, rope),
        grid=(b, l // tm),
        in_specs=specs,
        out_specs=[_rows(tm, nq), _rows(tm, nk), _rows(tm, nk)],
        out_shape=[jax.ShapeDtypeStruct((b, l, nq), BF16), jax.ShapeDtypeStruct((b, l, nk), BF16),
                   jax.ShapeDtypeStruct((b, l, nk), BF16)],
        compiler_params=_cparams("parallel", "parallel"),
        name="swa_proj_lat" if rope else "swa_proj_ctx",
    )(*ins)


def _fold(x, op):
    acc = x[:, :LANES]
    for t in range(1, x.shape[1] // LANES):
        acc = op(acc, x[:, t * LANES:(t + 1) * LANES])
    return acc


def _attend(chains, s_scr):
    def phase1(ci):
        q, parts, _ = chains[ci]
        st = {"mx": None}

        def make(p, off):
            def step():
                k_thunk, _, bias, nk = parts[p]
                s = _dot_t(q, k_thunk())
                if bias is not None:
                    s = s + bias
                s_scr[ci % 2, :, off:off + nk] = s
                f = _fold(s, jnp.maximum)
                st["mx"] = f if st["mx"] is None else jnp.maximum(st["mx"], f)
            return step

        steps, off = [], 0
        for p, part in enumerate(parts):
            steps.append(make(p, off))
            off += part[3]
        return st, steps

    def phase2(ci, st):
        _, parts, sink = chains[ci]
        out = {"acc": [None, None]}

        def make(p, off):
            def step():
                if p == 0:
                    m = jnp.max(st["mx"], axis=-1, keepdims=True)
                    out["m"] = m if sink is None else jnp.maximum(m, sink)
                nk = parts[p][3]
                pr = jnp.exp2(s_scr[ci % 2, :, off:off + nk] - out["m"])
                v = parts[p][1]()
                pv = _dot(pr.astype(BF16), jnp.concatenate([v, jnp.ones_like(v)], axis=1))
                out["acc"][p % 2] = pv if out["acc"][p % 2] is None else out["acc"][p % 2] + pv
            return step

        def finish():
            acc = out["acc"][0] if out["acc"][1] is None else out["acc"][0] + out["acc"][1]
            den = acc[:, LANES:]
            if sink is not None:
                den = den + jnp.exp2(sink - out["m"])
            return acc[:, :LANES] / den

        steps, off = [], 0
        for p, part in enumerate(parts):
            steps.append(make(p, off))
            off += part[3]
        return steps, finish

    results = []
    pending = None
    for ci in range(len(chains) + 1):
        st, steps1 = phase1(ci) if ci < len(chains) else (None, [])
        steps2 = pending[0] if pending else []
        for k in range(max(len(steps1), len(steps2))):
            if k < len(steps1):
                steps1[k]()
            if k < len(steps2):
                steps2[k]()
        if pending:
            results.append(pending[1]())
        pending = phase2(ci, st) if ci < len(chains) else None
    return results


def _swa_attn_body(has_win, tq, sub, seq, sink_ref, q_ref, *refs):
    if has_win:
        kd_ref, vd_ref, kc_ref, vc_ref, o_ref, s_scr = refs
    else:
        kc_ref, vc_ref, o_ref, s_scr = refs
    hk = pl.program_id(1)
    n_ctx = kc_ref.shape[1]
    span = sub + 2 * WINDOW
    low = lax.broadcasted_iota(jnp.int32, (sub, LANES), 1) < HEAD_DIM
    chains = []
    for r in range(tq // sub):
        if has_win:
            s0 = pl.program_id(2) * tq + r * sub
            ws = pl.multiple_of(jnp.clip(s0 - WINDOW, 0, seq - span), LANES)
            qpos = s0 + lax.broadcasted_iota(jnp.int32, (sub, span), 0)
            kpos = ws + lax.broadcasted_iota(jnp.int32, (sub, span), 1)
            bias = jnp.where(jnp.abs(qpos - kpos) <= WINDOW, 0.0, NEG_INF).astype(F32)
        for g in range(4):
            qp = q_ref[0, r * sub:(r + 1) * sub, (g // 2) * LANES:(g // 2 + 1) * LANES]
            qm = jnp.where(low if g % 2 == 0 else jnp.logical_not(low), qp, jnp.zeros_like(qp))
            parts = []
            if has_win:
                parts.append((lambda ws=ws: kd_ref[0, pl.ds(ws, span), :],
                              lambda ws=ws: vd_ref[0, pl.ds(ws, span), :], bias, span))
            parts.append((lambda: kc_ref[0], lambda: vc_ref[0], None, n_ctx))
            chains.append((qm, parts, sink_ref[hk * 4 + g]))
    o = _attend(chains, s_scr)
    for r in range(tq // sub):
        o_ref[0, r * sub:(r + 1) * sub, :] = jnp.concatenate(
            [jnp.where(low, o[4 * r], o[4 * r + 1]), jnp.where(low, o[4 * r + 2], o[4 * r + 3])],
            axis=1).astype(o_ref.dtype)


def _swa_attn(sink, q, kd, vd, kcd, vcd, tq, sub):
    b, l, dq = q.shape
    n_ctx = kcd.shape[1]
    has_win = kd is not None
    gw = dq // SWA_KV_HEADS
    ins = [sink, q]
    specs = [pl.BlockSpec(memory_space=pltpu.SMEM),
             pl.BlockSpec((1, tq, gw), lambda bb, h, i: (bb, i, h))]
    if has_win:
        ins += [kd, vd]
        specs += [pl.BlockSpec((1, l, LANES), lambda bb, h, i: (bb, 0, h))] * 2
    ins += [kcd, vcd]
    specs += [pl.BlockSpec((1, n_ctx, LANES), lambda bb, h, i: (bb, 0, h))] * 2
    return pl.pallas_call(
        functools.partial(_swa_attn_body, has_win, tq, sub, l),
        grid=(b, SWA_KV_HEADS, l // tq),
        in_specs=specs,
        out_specs=pl.BlockSpec((1, tq, gw), lambda bb, h, i: (bb, i, h)),
        out_shape=jax.ShapeDtypeStruct((b, l, dq), BF16),
        scratch_shapes=[pltpu.VMEM((2, sub, n_ctx + (sub + 2 * WINDOW if has_win else 0)), F32)],
        compiler_params=_cparams("parallel", "parallel", "parallel"),
        name="swa_attn_lat" if has_win else "swa_attn_ctx",
    )(*ins)


def _mla_proj_body(rope, x_ref, gain_ref, sh_ref, sc_ref, wd_ref, gql_ref, wuq_ref, gkvl_ref,
                   wuk_ref, wuv_ref, g128_ref, perm_ref, gkn_ref, cq_ref, ck_ref, *rest):
    if rope:
        sq_ref, sk_ref, q_out, k_out, v_out = rest
        sq, sk = sq_ref[...], sk_ref[...]
    else:
        q_out, k_out, v_out = rest
        sq = sk = None
    h = _modnorm(x_ref[0], gain_ref[...], sh_ref[0], sc_ref[0]).astype(BF16)
    t = _dot(h, wd_ref[...])
    cq = t[:, :Q_LORA]
    ckv = t[:, Q_LORA:Q_LORA + KV_LORA]
    kr = t[:, Q_LORA + KV_LORA:]
    cq = (cq * lax.rsqrt(jnp.mean(cq * cq, axis=-1, keepdims=True) + EPS) * gql_ref[...]).astype(BF16)
    ckv = (ckv * lax.rsqrt(jnp.mean(ckv * ckv, axis=-1, keepdims=True) + EPS) * gkvl_ref[...]).astype(BF16)
    qa = _dot(cq, wuq_ref[...])
    ka = _dot(ckv, wuk_ref[...])
    v_out[0] = _dot(ckv, wuv_ref[...]).astype(BF16)
    g128, perm = g128_ref[...], perm_ref[...]
    q = _norm_rope(qa, g128, perm, cq_ref[...], sq, MLA_QK * EPS).astype(BF16)
    kr_rot = kr * ck_ref[...]
    if rope:
        kr_rot = kr_rot + _dot(kr.astype(BF16), perm[:LANES, :LANES]) * sk
    width = ka.shape[1]
    kraw = ka + _lane_tile(kr, width)
    ssq = _chunk_dot((kraw * kraw).astype(BF16), g128)
    k = ((ka * _lane_tile(gkn_ref[...], width) + _lane_tile(kr_rot, width))
         * lax.rsqrt(ssq + MLA_QK * EPS)).astype(BF16)
    for hh in range(MLA_HEADS):
        q_out[0, hh] = q[:, hh * LANES:(hh + 1) * LANES]
        k_out[0, hh] = k[:, hh * LANES:(hh + 1) * LANES]


def _mla_proj(x, gain, sh, sc, w, tables, rope, tm):
    b, l, d = x.shape
    ins = [x, gain, sh, sc] + list(w) + list(tables)
    specs = ([_rows(tm, d), _const(gain.shape), _vec(d), _vec(d)] + [_const(a.shape) for a in w]
             + _table_specs(tables, rope, tm))
    hspec = pl.BlockSpec((1, MLA_HEADS, tm, LANES), lambda bb, i: (bb, 0, i, 0))
    hshape = jax.ShapeDtypeStruct((b, MLA_HEADS, l, LANES), BF16)
    nv = w[5].shape[1]
    return pl.pallas_call(
        functools.partial(_mla_proj_body, rope),
        grid=(b, l // tm),
        in_specs=specs,
        out_specs=[hspec, hspec, _rows(tm, nv)],
        out_shape=[hshape, hshape, jax.ShapeDtypeStruct((b, l, nv), BF16)],
        compiler_params=_cparams("parallel", "parallel"),
        name="mla_proj_lat" if rope else "mla_proj_ctx",
    )(*ins)


def _mla_attn_body(has_lat, tk, sub, q_ref, *refs):
    if has_lat:
        kl_ref, vl_ref, kc_ref, vc_ref, o_ref, s_scr = refs
    else:
        kc_ref, vc_ref, o_ref, s_scr = refs
    tq = q_ref.shape[2]
    n_ctx = kc_ref.shape[2]
    low = lax.broadcasted_iota(jnp.int32, (sub, LANES), 1) < MLA_NOPE
    chains = []
    for r in range(tq // sub):
        for e in range(2):
            parts = []
            if has_lat:
                for j in range(kl_ref.shape[2] // tk):
                    parts.append((lambda e=e, j=j: kl_ref[0, e, j * tk:(j + 1) * tk, :],
                                  lambda j=j: vl_ref[0, j * tk:(j + 1) * tk, :], None, tk))
            parts.append((lambda e=e: kc_ref[0, e], lambda: vc_ref[0], None, n_ctx))
            chains.append((q_ref[0, e, r * sub:(r + 1) * sub, :], parts, None))
    o = _attend(chains, s_scr)
    for r in range(tq // sub):
        o_ref[0, r * sub:(r + 1) * sub, :] = jnp.where(low, o[2 * r], o[2 * r + 1]).astype(o_ref.dtype)


def _mla_attn(q, kl, vl, kc, vc, tq, sub):
    b, nh, l, _ = q.shape
    n_ctx = kc.shape[2]
    has_lat = kl is not None
    ins = [q]
    specs = [pl.BlockSpec((1, 2, tq, LANES), lambda bb, j, i: (bb, j, i, 0))]
    if has_lat:
        ins += [kl, vl]
        specs += [pl.BlockSpec((1, 2, l, LANES), lambda bb, j, i: (bb, j, 0, 0)),
                  pl.BlockSpec((1, l, LANES), lambda bb, j, i: (bb, 0, j))]
    ins += [kc, vc]
    specs += [pl.BlockSpec((1, 2, n_ctx, LANES), lambda bb, j, i: (bb, j, 0, 0)),
              pl.BlockSpec((1, n_ctx, LANES), lambda bb, j, i: (bb, 0, j))]
    return pl.pallas_call(
        functools.partial(_mla_attn_body, has_lat, MLA_KEY_BLOCK, sub),
        grid=(b, nh // 2, l // tq),
        in_specs=specs,
        out_specs=pl.BlockSpec((1, tq, LANES), lambda bb, j, i: (bb, i, j)),
        out_shape=jax.ShapeDtypeStruct((b, l, nh * MLA_NOPE), BF16),
        scratch_shapes=[pltpu.VMEM((2, sub, n_ctx + (l if has_lat else 0)), F32)],
        compiler_params=_cparams("parallel", "parallel", "parallel"),
        name="mla_attn_lat" if has_lat else "mla_attn_ctx",
    )(*ins)


def _lru_in_body(x_ref, gain_ref, sh_ref, sc_ref, w_ref, g_out, x_out):
    h = _modnorm(x_ref[0], gain_ref[...], sh_ref[0], sc_ref[0]).astype(BF16)
    u = _dot(h, w_ref[...])
    width = u.shape[1] // 2
    g_out[0] = jax.nn.gelu(u[:, :width], approximate=True).astype(BF16)
    x_out[0] = u[:, width:]


def _lru_in(x, gain, sh, sc, w_in, tm):
    b, l, d = x.shape
    width = w_in.shape[1] // 2
    return pl.pallas_call(
        _lru_in_body,
        grid=(b, l // tm),
        in_specs=[_rows(tm, d), _const(gain.shape), _vec(d), _vec(d), _const(w_in.shape)],
        out_specs=[_rows(tm, width), _rows(tm, width)],
        out_shape=[jax.ShapeDtypeStruct((b, l, width), BF16), jax.ShapeDtypeStruct((b, l, width), F32)],
        compiler_params=_cparams("parallel", "parallel"),
        name="lru_in",
    )(x, gain, sh, sc, w_in)


def _halo_specs(tm, l, d, halo=HALO):
    nb = tm // halo
    last = l // halo - 1
    prev = pl.BlockSpec((1, halo, d), lambda b, i: (b, jnp.maximum(i * nb - 1, 0), 0))
    nxt = pl.BlockSpec((1, halo, d), lambda b, i: (b, jnp.minimum((i + 1) * nb, last), 0))
    return prev, nxt


def _lru_coef_body(x_ref, xp_ref, xn_ref, cw_ref, gw_ref, gb_ref, lam_ref, af_out, bf_out, ar_out, br_out):
    i = pl.program_id(1)
    tm = x_ref.shape[1]
    x = x_ref[0]
    pm = (i > 0).astype(F32)
    nm = (i < pl.num_programs(1) - 1).astype(F32)
    p2 = xp_ref[0, HALO - 2:HALO - 1, :] * pm
    p1 = xp_ref[0, HALO - 1:HALO, :] * pm
    n1 = xn_ref[0, 0:1, :] * nm
    row = lax.broadcasted_iota(jnp.int32, x.shape, 0)
    xm1 = jnp.where(row == 0, p1, pltpu.roll(x, 1, 0))
    xm2 = jnp.where(row == 0, p2, jnp.where(row == 1, p1, pltpu.roll(x, 2, 0)))
    xp1 = jnp.where(row == tm - 1, n1, pltpu.roll(x, tm - 1, 0))
    cw = cw_ref[...]
    xc = cw[0:1] * xm2 + cw[1:2] * xm1 + cw[2:3] * x + cw[3:4] * xp1 + cw[4:5]
    xcb = xc.astype(BF16)
    outs = ((af_out, bf_out), (ar_out, br_out))
    for d in range(2):
        gates = []
        for k in range(2):
            g = jnp.concatenate(
                [_dot(xcb[:, n * LRU_BW:(n + 1) * LRU_BW], gw_ref[(d * 2 + k) * LRU_BLOCKS + n])
                 for n in range(LRU_BLOCKS)], axis=1)
            gates.append(jax.nn.sigmoid(g + gb_ref[d * 2 + k:d * 2 + k + 1, :]))
        r, ig = gates
        nl = -lam_ref[d:d + 1, :]
        softplus = jnp.maximum(nl, 0.0) + jnp.log1p(jnp.exp(-jnp.abs(nl)))
        log_a = (-LRU_C) * r * softplus
        a = jnp.exp(log_a)
        outs[d][0][0] = a
        outs[d][1][0] = jnp.sqrt(-jnp.tanh(log_a) * (a * a + 1.0)) * (ig * xc)


def _lru_coef(xr, cw, gw, gb, lam, tm):
    b, l, d = xr.shape
    prev, nxt = _halo_specs(tm, l, d)
    out = jax.ShapeDtypeStruct((b, l, d), F32)
    return pl.pallas_call(
        _lru_coef_body,
        grid=(b, l // tm),
        in_specs=[_rows(tm, d), prev, nxt, _const(cw.shape), _const(gw.shape), _const(gb.shape),
                  _const(lam.shape)],
        out_specs=[_rows(tm, d)] * 4,
        out_shape=[out] * 4,
        compiler_params=_cparams("parallel", "parallel"),
        name="lru_coef",
    )(xr, xr, xr, cw, gw, gb, lam)


def _scan_body(reverse, has_add, a_ref, b_ref, h0_ref, *rest):
    if has_add:
        add_ref, o_ref, he_ref, h_s, a_s, b_s = rest
    else:
        o_ref, he_ref, h_s, a_s, b_s = rest
    tm, width = a_ref.shape[1], a_ref.shape[2]
    n_groups = tm // SUBLANES

    @pl.when(pl.program_id(1) == 0)
    def _():
        h_s[...] = jnp.broadcast_to(h0_ref[0], h_s.shape)

    a = a_ref[0].reshape(n_groups, SUBLANES, width)
    b = b_ref[0].reshape(n_groups, SUBLANES, width)
    in_group = lax.broadcasted_iota(jnp.int32, a.shape, 1)
    for k in (1, 2, 4):
        if reverse:
            a_sh, b_sh = pltpu.roll(a, SUBLANES - k, 1), pltpu.roll(b, SUBLANES - k, 1)
            valid = in_group < SUBLANES - k
        else:
            a_sh, b_sh = pltpu.roll(a, k, 1), pltpu.roll(b, k, 1)
            valid = in_group >= k
        b = jnp.where(valid, b + a * b_sh, b)
        a = jnp.where(valid, a * a_sh, a)
    a_s[...] = a.reshape(tm, width)
    b_s[...] = b.reshape(tm, width)

    def group(j, carry):
        g = n_groups - 1 - j if reverse else j
        rows = pl.ds(pl.multiple_of(g * SUBLANES, SUBLANES), SUBLANES)
        h = a_s[rows, :] * carry + b_s[rows, :]
        o_ref[0, rows, :] = h + add_ref[0, rows, :] if has_add else h
        last = h[0:1, :] if reverse else h[SUBLANES - 1:SUBLANES, :]
        return jnp.broadcast_to(last, carry.shape)

    carry = lax.fori_loop(0, n_groups, group, h_s[...], unroll=8)
    h_s[...] = carry
    he_ref[0] = carry[0:1, :]


def _scan(a, b, h0, add, reverse, tm):
    nb, t, width = a.shape
    nt = t // tm
    imap = (lambda bb, i: (bb, nt - 1 - i, 0)) if reverse else (lambda bb, i: (bb, i, 0))
    blk = pl.BlockSpec((1, tm, width), imap)
    hspec = pl.BlockSpec((1, 1, width), lambda bb, i: (bb, 0, 0))
    ins = [a, b, h0] + ([add] if add is not None else [])
    specs = [blk, blk, hspec] + ([blk] if add is not None else [])
    return pl.pallas_call(
        functools.partial(_scan_body, reverse, add is not None),
        grid=(nb, nt),
        in_specs=specs,
        out_specs=[blk, hspec],
        out_shape=[jax.ShapeDtypeStruct((nb, t, width), F32), jax.ShapeDtypeStruct((nb, 1, width), F32)],
        scratch_shapes=[pltpu.VMEM((SUBLANES, width), F32), pltpu.VMEM((tm, width), F32),
                        pltpu.VMEM((tm, width), F32)],
        compiler_params=_cparams("parallel", "arbitrary"),
        name="lru_scan_rev" if reverse else "lru_scan_fwd",
    )(*ins)


def _block_tail_body(has_r, x_ref, xp_ref, xn_ref, o_ref, op_ref, on_ref, *rest):
    if has_r:
        r_ref, rp_ref, rn_ref = rest[:3]
        rest = rest[3:]
    (wo_ref, g1_ref, gain_ref, sh_ref, sc_ref, g2_ref, wu_ref, cw_ref, wd_ref, y_ref, x1_s, h_s, act_s) = rest
    i = pl.program_id(1)
    tm = x_ref.shape[1]
    d_ff = wd_ref.shape[0]
    gain, sh, sc, g1 = gain_ref[...], sh_ref[0], sc_ref[0], g1_ref[0]

    def mixed(o, r):
        return o if r is None else (o.astype(F32) * r).astype(BF16)

    wo = wo_ref[...]
    x1 = x_ref[0] + g1 * _dot(mixed(o_ref[0], r_ref[0] if has_r else None), wo)
    x1_s[...] = x1
    h_s[0:tm, :] = _modnorm(x1, gain, sh, sc).astype(BF16)
    o_halo = jnp.concatenate([op_ref[0], on_ref[0]], axis=0)
    r_halo = jnp.concatenate([rp_ref[0], rn_ref[0]], axis=0) if has_r else None
    x_halo = jnp.concatenate([xp_ref[0], xn_ref[0]], axis=0)
    x1_halo = x_halo + g1 * _dot(mixed(o_halo, r_halo), wo)
    h_s[tm:tm + 2 * FFN_HALO, :] = _modnorm(x1_halo, gain, sh, sc).astype(BF16)
    pm = (i > 0).astype(F32)
    nm = (i < pl.num_programs(1) - 1).astype(F32)

    def up(c):
        lo = c * FF_CHUNK
        return (_dot(h_s[...], wu_ref[:, lo:lo + FF_CHUNK]),
                _dot(h_s[0:tm, :], wu_ref[:, d_ff + lo:d_ff + lo + FF_CHUNK]))

    nxt = up(0)
    for c in range(d_ff // FF_CHUNK):
        ge, val = nxt
        if (c + 1) * FF_CHUNK < d_ff:
            nxt = up(c + 1)
        lo = c * FF_CHUNK
        g = ge[0:tm]
        gp = ge[tm + FFN_HALO - 1:tm + FFN_HALO] * pm
        gn = ge[tm + FFN_HALO:tm + FFN_HALO + 1] * nm
        row = lax.broadcasted_iota(jnp.int32, g.shape, 0)
        gd = jnp.where(row == 0, gp, pltpu.roll(g, 1, 0))
        gu = jnp.where(row == tm - 1, gn, pltpu.roll(g, tm - 1, 0))
        cw = cw_ref[:, lo:lo + FF_CHUNK]
        y = cw[0:1] * gd + cw[1:2] * g + cw[2:3] * gu + cw[3:4]
        act_s[:, lo:lo + FF_CHUNK] = (y * jax.nn.sigmoid(y) * val).astype(BF16)
    y_ref[0] = x1_s[...] + g2_ref[0] * _dot(act_s[...], wd_ref[...])


def _block_tail(x, o, r, w_o, g1, gain, sh, sc, g2, w, tm):
    b, l, d = x.shape
    dk = o.shape[2]
    wu, cw, wd = w
    xs = [_rows(tm, d), *_halo_specs(tm, l, d, FFN_HALO)]
    os_ = [_rows(tm, dk), *_halo_specs(tm, l, dk, FFN_HALO)]
    ins = [x, x, x, o, o, o] + ([r, r, r] if r is not None else [])
    specs = xs + os_ + (os_ if r is not None else [])
    ins += [w_o, g1, gain, sh, sc, g2, wu, cw, wd]
    specs += [_const(w_o.shape), _vec(d), _const(gain.shape), _vec(d), _vec(d), _vec(d),
              _const(wu.shape), _const(cw.shape), _const(wd.shape)]
    return pl.pallas_call(
        functools.partial(_block_tail_body, r is not None),
        grid=(b, l // tm),
        in_specs=specs,
        out_specs=_rows(tm, d),
        out_shape=jax.ShapeDtypeStruct((b, l, d), F32),
        scratch_shapes=[pltpu.VMEM((tm, d), F32), pltpu.VMEM((tm + 2 * FFN_HALO, d), BF16),
                        pltpu.VMEM((tm, wd.shape[0]), BF16)],
        compiler_params=_cparams("parallel", "parallel"),
        name="block_tail",
    )(*ins)


def _rope_cs(n_tokens, rot_dim):
    pos = jnp.arange(n_tokens)
    row = (pos // GRID_W).astype(F32)
    col = (pos % GRID_W).astype(F32)
    n_freq = rot_dim // 4
    inv = ROPE_BASE ** (-jnp.arange(n_freq, dtype=F32) / n_freq)
    ar, ac = row[:, None] * inv, col[:, None] * inv
    cos = jnp.concatenate([jnp.cos(ar), jnp.cos(ar), jnp.cos(ac), jnp.cos(ac)], axis=1)
    sin = jnp.concatenate([-jnp.sin(ar), jnp.sin(ar), -jnp.sin(ac), jnp.sin(ac)], axis=1)
    return cos, sin


def _pad_rows(a, rows):
    return jnp.pad(a, ((0, rows - a.shape[0]), (0, 0)))


def _pair_swap(n, half):
    i = jnp.arange(n)
    return jnp.where((i & half) == 0, i + half, i - half)


def _perm_matrix(half):
    return (jnp.arange(MXU_COLS)[:, None] == _pair_swap(MXU_COLS, half)[None, :]).astype(BF16)


def _group_ones(size):
    g = jnp.arange(MXU_COLS) // size
    return (g[:, None] == g[None, :]).astype(BF16)


def _gain_tables(gain, cos, sin, half):
    return gain[None, :] * cos, gain[_pair_swap(LANES, half)][None, :] * sin, gain[None, :]


def kernel(x, c, ctx, c_ctx, norm1, norm2, mod_w, mod_b, swa_w_qkv, swa_q_gain, swa_k_gain, swa_sink, swa_w_o, mla_w_down, mla_q_lora_gain, mla_w_uq, mla_kv_lora_gain, mla_w_uk, mla_w_uv, mla_q_gain, mla_k_gain, mla_w_o, lru_w_in, lru_conv_w, lru_conv_b, lru_gate_w, lru_gate_b, lru_lam, lru_w_out, ffn_w_up, ffn_conv_w, ffn_conv_b, ffn_w_down):
    bsz, seq, d = x.shape
    n_ctx = ctx.shape[1]
    depth = norm1.shape[0]
    tm_lat, tm_ctx = 512, n_ctx

    cond = _pad_rows(jnp.concatenate([c, c_ctx[None, :]], axis=0), 16)
    mods = _mod_all(cond, mod_w, mod_b)

    cos_s, sin_s = _rope_cs(seq, HEAD_DIM)
    cos_s, sin_s = jnp.tile(cos_s, (1, 2)), jnp.tile(sin_s, (1, 2))
    cos_m, sin_m = _rope_cs(seq, MLA_ROPE)
    one = jnp.ones((seq, MLA_NOPE), F32)
    zero = jnp.zeros((seq, MLA_NOPE), F32)
    cos_m = jnp.concatenate([one, cos_m, one[:, :LANES - MLA_QK]], axis=1)
    sin_m = jnp.concatenate([zero, sin_m, zero[:, :LANES - MLA_QK]], axis=1)
    ones64, ones128 = _group_ones(HEAD_DIM), _group_ones(LANES)

    for layer in range(depth):
        kind, idx = layer % 3, layer // 3
        with_ctx_out = layer < depth - 1
        m = mods[layer].reshape(16, 6, d)
        ml = [m[:bsz, j][:, None, :] for j in range(6)]
        mc = [jnp.broadcast_to(m[bsz, j][None, None, :], (bsz, 1, d)) for j in range(6)]
        n1 = norm1[layer][None, :]
        n2 = norm2[layer][None, :]
        r_lat = r_ctx = None
        if kind == 0:
            wqkv = swa_w_qkv[idx]
            nq = SWA_HEADS * HEAD_DIM
            nkv = SWA_KV_HEADS * HEAD_DIM

            def dup(w):
                w = w.reshape(d, SWA_KV_HEADS, 1, HEAD_DIM)
                return jnp.broadcast_to(w, (d, SWA_KV_HEADS, 2, HEAD_DIM)).reshape(d, 2 * nkv).astype(BF16)

            w = (wqkv[:, :nq].astype(BF16), dup(wqkv[:, nq:nq + nkv]), dup(wqkv[:, nq + nkv:]),
                 ones64, ones128, _perm_matrix(HEAD_DIM // 4))
            cq, sq, gq = _gain_tables(jnp.tile(swa_q_gain[idx], 2) * LOG2E, cos_s, sin_s, HEAD_DIM // 4)
            ck, sk, gk = _gain_tables(jnp.tile(swa_k_gain[idx], 2) * (2 * HEAD_DIM) ** 0.5, cos_s, sin_s,
                                      HEAD_DIM // 4)
            sink = swa_sink[idx] * LOG2E
            qc, kc, vc = _swa_proj(ctx, n1, mc[0], mc[1], w, (gq, gk), False, tm_ctx)
            ql, kl, vl = _swa_proj(x, n1, ml[0], ml[1], w, (cq, ck, sq, sk), True, tm_lat)
            o_lat = _swa_attn(sink, ql, kl, vl, kc, vc, 1024, 256)
            o_ctx = _swa_attn(sink, qc, None, None, kc, vc, n_ctx, n_ctx) if with_ctx_out else None
            w_o = swa_w_o[idx].astype(BF16)
        elif kind == 1:
            wdn = mla_w_down[idx]
            zc = jnp.zeros((d, MLA_NOPE), F32)
            wd_pad = jnp.concatenate([wdn[:, :Q_LORA + KV_LORA], zc, wdn[:, Q_LORA + KV_LORA:],
                                      zc[:, :LANES - MLA_QK]], axis=1).astype(BF16)
            wuq = jnp.pad(mla_w_uq[idx].reshape(Q_LORA, MLA_HEADS, MLA_QK),
                          ((0, 0), (0, 0), (0, LANES - MLA_QK))).reshape(Q_LORA, MLA_HEADS * LANES).astype(BF16)
            wuk = jnp.pad(mla_w_uk[idx].reshape(KV_LORA, MLA_HEADS, MLA_NOPE),
                          ((0, 0), (0, 0), (0, LANES - MLA_NOPE))).reshape(KV_LORA, MLA_HEADS * LANES).astype(BF16)
            root = MLA_QK ** 0.5
            gq = jnp.pad(mla_q_gain[idx] * LOG2E, (0, LANES - MLA_QK))
            gk_nope = jnp.pad(mla_k_gain[idx][:MLA_NOPE] * root, (0, LANES - MLA_NOPE))[None, :]
            gk_rope = jnp.pad(mla_k_gain[idx][MLA_NOPE:] * root, (MLA_NOPE, LANES - MLA_QK))
            cq, sq, gq = _gain_tables(gq, cos_m, sin_m, MLA_ROPE // 4)
            ck, sk, gk = _gain_tables(gk_rope, cos_m, sin_m, MLA_ROPE // 4)
            w = (wd_pad, mla_q_lora_gain[idx][None, :], wuq, mla_kv_lora_gain[idx][None, :], wuk,
                 mla_w_uv[idx].astype(BF16), ones128, _perm_matrix(MLA_ROPE // 4), gk_nope)
            qc, kc, vc = _mla_proj(ctx, n1, mc[0], mc[1], w, (gq, gk), False, tm_ctx)
            ql, kl, vl = _mla_proj(x, n1, ml[0], ml[1], w, (cq, ck, sq, sk), True, tm_lat)
            o_lat = _mla_attn(ql, kl, vl, kc, vc, 1024, 256)
            o_ctx = _mla_attn(qc, None, None, kc, vc, n_ctx, n_ctx) if with_ctx_out else None
            w_o = mla_w_o[idx].astype(BF16)
        else:
            w_in = lru_w_in[idx].astype(BF16)
            cw = _pad_rows(jnp.concatenate([lru_conv_w[idx], lru_conv_b[idx][None, :]], axis=0), 8)
            gw = lru_gate_w[idx].reshape(2 * 2 * LRU_BLOCKS, LRU_BW, LRU_BW).astype(BF16)
            gb = _pad_rows(lru_gate_b[idx].reshape(4, d), 8)
            lam = _pad_rows(lru_lam[idx], 8)
            g_ctx, xr_ctx = _lru_in(ctx, n1, mc[0], mc[1], w_in, tm_ctx)
            g_lat, xr_lat = _lru_in(x, n1, ml[0], ml[1], w_in, tm_lat)
            afc, bfc, arc, brc = _lru_coef(xr_ctx, cw, gw, gb, lam, tm_ctx)
            afl, bfl, arl, brl = _lru_coef(xr_lat, cw, gw, gb, lam, tm_lat)
            h0 = jnp.zeros((bsz, 1, d), F32)
            sfc, hfc = _scan(afc, bfc, h0, None, False, tm_ctx)
            r_ctx, hrc = _scan(arc, brc, h0, sfc, True, tm_ctx)
            sfl, _ = _scan(afl, bfl, hfc, None, False, tm_lat)
            r_lat, _ = _scan(arl, brl, hrc, sfl, True, tm_lat)
            o_lat, o_ctx = g_lat, (g_ctx if with_ctx_out else None)
            w_o = lru_w_out[idx].astype(BF16)

        cw_ffn = _pad_rows(jnp.concatenate([ffn_conv_w[layer], ffn_conv_b[layer][None, :]], axis=0), 8)
        wf = (ffn_w_up[layer].astype(BF16), cw_ffn, ffn_w_down[layer].astype(BF16))

        x = _block_tail(x, o_lat, r_lat, w_o, ml[2], n2, ml[3], ml[4], ml[5], wf, tm_lat)
        if with_ctx_out:
            ctx = _block_tail(ctx, o_ctx, r_ctx, w_o, mc[2], n2, mc[3], mc[4], mc[5], wf, tm_ctx)
    return x
```

```python
import functools

import jax
import jax.numpy as jnp
from jax import lax
from jax.experimental import pallas as pl
from jax.experimental.pallas import tpu as pltpu

F32 = jnp.float32
BF16 = jnp.bfloat16

GRID_W = 64
HEAD_DIM = 64
SWA_HEADS = 16
SWA_KV_HEADS = 4
WINDOW = 128
MLA_HEADS = 16
MLA_NOPE = 64
MLA_ROPE = 32
MLA_QK = MLA_NOPE + MLA_ROPE
Q_LORA = 384
KV_LORA = 256
LRU_BLOCKS = 4
LRU_BW = 256
LRU_C = 8.0
D_FF = 2816
ROPE_BASE = 10000.0
EPS = 1e-6
NEG_INF = -1e30
LOG2E = 1.4426950408889634

LANES = 128
MXU_COLS = 256
FF_CHUNK = MXU_COLS
SUBLANES = 8
HALO = SUBLANES
FFN_HALO = 16
MLA_KEY_BLOCK = 512
ROW_TILE = 512
ATTN_ROWS = MXU_COLS
ATTN_Q_TILE = 1024
VMEM_LIMIT = 56 * 1024 * 1024


def _cparams(*sem):
    return pltpu.CompilerParams(dimension_semantics=sem, vmem_limit_bytes=VMEM_LIMIT)


def _const(shape):
    nd = len(shape)
    return pl.BlockSpec(shape, lambda *_: (0,) * nd, pipeline_mode=pl.Buffered(1))


def _rows(tm, d):
    return pl.BlockSpec((1, tm, d), lambda b, i: (b, i, 0))


def _vec(d):
    return pl.BlockSpec((1, 1, d), lambda b, i: (b, 0, 0))


def _dot(a, b):
    return jnp.dot(a, b, preferred_element_type=F32)


def _dot_t(a, b):
    return lax.dot_general(a, b, (((1,), (1,)), ((), ())), preferred_element_type=F32)


def _modnorm(x, gain, shift, scale):
    ms = jnp.mean(x * x, axis=-1, keepdims=True)
    y = x * lax.rsqrt(ms + EPS) * gain
    return y * (1.0 + scale) + shift


def _chunk_dot(x, m):
    w = m.shape[0]
    return jnp.concatenate([_dot(x[:, j * w:(j + 1) * w], m) for j in range(x.shape[1] // w)], axis=1)


def _lane_tile(t, width):
    return jnp.concatenate([t] * (width // t.shape[1]), axis=1)


def _norm_rope(x, group_ones, perm, cos, sin, eps_sum):
    ssq = _chunk_dot((x * x).astype(BF16), group_ones)
    y = x * _lane_tile(cos, x.shape[1])
    if sin is not None:
        y = y + _chunk_dot(x.astype(BF16), perm) * _lane_tile(sin, x.shape[1])
    return y * lax.rsqrt(ssq + eps_sum)


def _mod_body(c_ref, w_ref, b_ref, o_ref):
    c = c_ref[...]
    cs = (c * jax.nn.sigmoid(c)).astype(BF16)
    o_ref[0] = _dot(cs, w_ref[0].astype(BF16)) + b_ref[0]


def _mod_all(cond, mod_w, mod_b):
    depth, d, n = mod_w.shape
    tn = 1536
    rows = cond.shape[0]
    return pl.pallas_call(
        _mod_body,
        grid=(depth, n // tn),
        in_specs=[pl.BlockSpec((rows, d), lambda l, j: (0, 0)),
                  pl.BlockSpec((1, d, tn), lambda l, j: (l, 0, j)),
                  pl.BlockSpec((1, 1, tn), lambda l, j: (l, 0, j))],
        out_specs=pl.BlockSpec((1, rows, tn), lambda l, j: (l, 0, j)),
        out_shape=jax.ShapeDtypeStruct((depth, rows, n), F32),
        compiler_params=_cparams("arbitrary", "arbitrary"),
        name="mod",
    )(cond, mod_w, mod_b.reshape(depth, 1, n))


def _swa_proj_body(rope, x_ref, gain_ref, sh_ref, sc_ref, wq_ref, wk_ref, wv_ref, g64_ref, g128_ref,
                   perm_ref, cq_ref, ck_ref, *rest):
    if rope:
        sq_ref, sk_ref, q_out, k_out, v_out = rest
        sq, sk = sq_ref[...], sk_ref[...]
    else:
        q_out, k_out, v_out = rest
        sq = sk = None
    h = _modnorm(x_ref[0], gain_ref[...], sh_ref[0], sc_ref[0]).astype(BF16)
    q = _dot(h, wq_ref[...])
    k = _dot(h, wk_ref[...])
    v_out[0] = _dot(h, wv_ref[...]).astype(BF16)
    perm = perm_ref[...]
    q_out[0] = _norm_rope(q, g64_ref[...], perm, cq_ref[...], sq, HEAD_DIM * EPS).astype(BF16)
    k_out[0] = _norm_rope(k, g128_ref[...], perm, ck_ref[...], sk, 2 * HEAD_DIM * EPS).astype(BF16)


def _table_specs(tables, rope, tm):
    if rope:
        return [pl.BlockSpec((tm, LANES), lambda bb, i: (i, 0))] * len(tables)
    return [_const(t.shape) for t in tables]


def _swa_proj(x, gain, sh, sc, w, tables, rope, tm):
    b, l, d = x.shape
    wq, wk, wv = w[:3]
    ins = [x, gain, sh, sc] + list(w) + list(tables)
    specs = ([_rows(tm, d), _const(gain.shape), _vec(d), _vec(d)] + [_const(a.shape) for a in w]
             + _table_specs(tables, rope, tm))
    nq, nk = wq.shape[1], wk.shape[1]
    return pl.pallas_call(
        functools.partial(_swa_proj_body, rope),
        grid=(b, l // tm),
        in_specs=specs,
        out_specs=[_rows(tm, nq), _rows(tm, nk), _rows(tm, nk)],
        out_shape=[jax.ShapeDtypeStruct((b, l, nq), BF16), jax.ShapeDtypeStruct((b, l, nk), BF16),
                   jax.ShapeDtypeStruct((b, l, nk), BF16)],
        compiler_params=_cparams("parallel", "parallel"),
        name="swa_proj_lat" if rope else "swa_proj_ctx",
    )(*ins)


def _fold(x, op):
    acc = x[:, :LANES]
    for t in range(1, x.shape[1] // LANES):
        acc = op(acc, x[:, t * LANES:(t + 1) * LANES])
    return acc


def _attend(chains, s_scr):
    def phase1(ci):
        q, parts, _ = chains[ci]
        st = {"mx": None}

        def make(p, off):
            def step():
                k_thunk, _, bias, nk = parts[p]
                s = _dot_t(q, k_thunk())
                if bias is not None:
                    s = s + bias()
                s_scr[ci % 2, :, off:off + nk] = s
                f = _fold(s, jnp.maximum)
                st["mx"] = f if st["mx"] is None else jnp.maximum(st["mx"], f)
            return step

        steps, off = [], 0
        for p, part in enumerate(parts):
            steps.append(make(p, off))
            off += part[3]
        return st, steps

    def phase2(ci, st):
        _, parts, sink = chains[ci]
        out = {"acc": [None, None]}

        def make(p, off):
            def step():
                if p == 0:
                    m = jnp.max(st["mx"], axis=-1, keepdims=True)
                    out["m"] = m if sink is None else jnp.maximum(m, sink)
                nk = parts[p][3]
                pr = jnp.exp2(s_scr[ci % 2, :, off:off + nk] - out["m"])
                v = parts[p][1]()
                pv = _dot(pr.astype(BF16), jnp.concatenate([v, jnp.ones_like(v)], axis=1))
                out["acc"][p % 2] = pv if out["acc"][p % 2] is None else out["acc"][p % 2] + pv
            return step

        def finish():
            acc = out["acc"][0] if out["acc"][1] is None else out["acc"][0] + out["acc"][1]
            den = acc[:, LANES:]
            if sink is not None:
                den = den + jnp.exp2(sink - out["m"])
            return acc[:, :LANES] / den

        steps, off = [], 0
        for p, part in enumerate(parts):
            steps.append(make(p, off))
            off += part[3]
        return steps, finish

    results = []
    pending = None
    for ci in range(len(chains) + 1):
        st, steps1 = phase1(ci) if ci < len(chains) else (None, [])
        steps2 = pending[0] if pending else []
        for k in range(max(len(steps1), len(steps2))):
            if k < len(steps1):
                steps1[k]()
            if k < len(steps2):
                steps2[k]()
        if pending:
            results.append(pending[1]())
        pending = phase2(ci, st) if ci < len(chains) else None
    return results


def _swa_attn_body(has_win, tq, sub, seq, sink_ref, q_ref, *refs):
    if has_win:
        bias_ref, kd_ref, vd_ref, kc_ref, vc_ref, o_ref, s_scr = refs
    else:
        kc_ref, vc_ref, o_ref, s_scr = refs
    hk = pl.program_id(1)
    n_ctx = kc_ref.shape[1]
    span = sub + 2 * WINDOW
    low = lax.broadcasted_iota(jnp.int32, (sub, LANES), 1) < HEAD_DIM
    chains = []
    for r in range(tq // sub):
        if has_win:
            s0 = pl.program_id(2) * tq + r * sub
            ws = pl.multiple_of(jnp.clip(s0 - WINDOW, 0, seq - span), LANES)
            which = (s0 - ws) // WINDOW

            def bias(which=which):
                return bias_ref[which]
        for g in range(4):
            qp = q_ref[0, r * sub:(r + 1) * sub, (g // 2) * LANES:(g // 2 + 1) * LANES]
            qm = jnp.where(low if g % 2 == 0 else jnp.logical_not(low), qp, jnp.zeros_like(qp))
            parts = []
            if has_win:
                parts.append((lambda ws=ws: kd_ref[0, pl.ds(ws, span), :],
                              lambda ws=ws: vd_ref[0, pl.ds(ws, span), :], bias, span))
            parts.append((lambda: kc_ref[0], lambda: vc_ref[0], None, n_ctx))
            chains.append((qm, parts, sink_ref[hk * 4 + g]))
    o = _attend(chains, s_scr)
    for r in range(tq // sub):
        o_ref[0, r * sub:(r + 1) * sub, :] = jnp.concatenate(
            [jnp.where(low, o[4 * r], o[4 * r + 1]), jnp.where(low, o[4 * r + 2], o[4 * r + 3])],
            axis=1).astype(o_ref.dtype)


def _swa_attn(sink, q, kd, vd, kcd, vcd, tq, sub):
    b, l, dq = q.shape
    n_ctx = kcd.shape[1]
    has_win = kd is not None
    gw = dq // SWA_KV_HEADS
    ins = [sink, q]
    specs = [pl.BlockSpec(memory_space=pltpu.SMEM),
             pl.BlockSpec((1, tq, gw), lambda bb, h, i: (bb, i, h))]
    if has_win:
        span = sub + 2 * WINDOW
        rel = (jnp.arange(3)[:, None, None] * WINDOW + jnp.arange(sub)[None, :, None]
               - jnp.arange(span)[None, None, :])
        band = jnp.where(jnp.abs(rel) <= WINDOW, 0.0, NEG_INF).astype(F32)
        ins += [band, kd, vd]
        specs += [_const(band.shape)] + [pl.BlockSpec((1, l, LANES), lambda bb, h, i: (bb, 0, h))] * 2
    ins += [kcd, vcd]
    specs += [pl.BlockSpec((1, n_ctx, LANES), lambda bb, h, i: (bb, 0, h))] * 2
    return pl.pallas_call(
        functools.partial(_swa_attn_body, has_win, tq, sub, l),
        grid=(b, SWA_KV_HEADS, l // tq),
        in_specs=specs,
        out_specs=pl.BlockSpec((1, tq, gw), lambda bb, h, i: (bb, i, h)),
        out_shape=jax.ShapeDtypeStruct((b, l, dq), BF16),
        scratch_shapes=[pltpu.VMEM((2, sub, n_ctx + (sub + 2 * WINDOW if has_win else 0)), F32)],
        compiler_params=_cparams("parallel", "parallel", "parallel"),
        name="swa_attn_lat" if has_win else "swa_attn_ctx",
    )(*ins)


def _mla_proj_body(rope, x_ref, gain_ref, sh_ref, sc_ref, wd_ref, gql_ref, wuq_ref, gkvl_ref,
                   wuk_ref, wuv_ref, g128_ref, perm_ref, gkn_ref, cq_ref, ck_ref, *rest):
    if rope:
        sq_ref, sk_ref, q_out, k_out, v_out = rest
        sq, sk = sq_ref[...], sk_ref[...]
    else:
        q_out, k_out, v_out = rest
        sq = sk = None
    h = _modnorm(x_ref[0], gain_ref[...], sh_ref[0], sc_ref[0]).astype(BF16)
    t = _dot(h, wd_ref[...])
    cq = t[:, :Q_LORA]
    ckv = t[:, Q_LORA:Q_LORA + KV_LORA]
    kr = t[:, Q_LORA + KV_LORA:]
    cq = (cq * lax.rsqrt(jnp.mean(cq * cq, axis=-1, keepdims=True) + EPS) * gql_ref[...]).astype(BF16)
    ckv = (ckv * lax.rsqrt(jnp.mean(ckv * ckv, axis=-1, keepdims=True) + EPS) * gkvl_ref[...]).astype(BF16)
    qa = _dot(cq, wuq_ref[...])
    ka = _dot(ckv, wuk_ref[...])
    v_out[0] = _dot(ckv, wuv_ref[...]).astype(BF16)
    g128, perm = g128_ref[...], perm_ref[...]
    q = _norm_rope(qa, g128, perm, cq_ref[...], sq, MLA_QK * EPS).astype(BF16)
    kr_rot = kr * ck_ref[...]
    if rope:
        kr_rot = kr_rot + _dot(kr.astype(BF16), perm[:LANES, :LANES]) * sk
    width = ka.shape[1]
    kraw = ka + _lane_tile(kr, width)
    ssq = _chunk_dot((kraw * kraw).astype(BF16), g128)
    k = ((ka * _lane_tile(gkn_ref[...], width) + _lane_tile(kr_rot, width))
         * lax.rsqrt(ssq + MLA_QK * EPS)).astype(BF16)
    for hh in range(MLA_HEADS):
        q_out[0, hh] = q[:, hh * LANES:(hh + 1) * LANES]
        k_out[0, hh] = k[:, hh * LANES:(hh + 1) * LANES]


def _mla_proj(x, gain, sh, sc, w, tables, rope, tm):
    b, l, d = x.shape
    ins = [x, gain, sh, sc] + list(w) + list(tables)
    specs = ([_rows(tm, d), _const(gain.shape), _vec(d), _vec(d)] + [_const(a.shape) for a in w]
             + _table_specs(tables, rope, tm))
    hspec = pl.BlockSpec((1, MLA_HEADS, tm, LANES), lambda bb, i: (bb, 0, i, 0))
    hshape = jax.ShapeDtypeStruct((b, MLA_HEADS, l, LANES), BF16)
    nv = w[5].shape[1]
    return pl.pallas_call(
        functools.partial(_mla_proj_body, rope),
        grid=(b, l // tm),
        in_specs=specs,
        out_specs=[hspec, hspec, _rows(tm, nv)],
        out_shape=[hshape, hshape, jax.ShapeDtypeStruct((b, l, nv), BF16)],
        compiler_params=_cparams("parallel", "parallel"),
        name="mla_proj_lat" if rope else "mla_proj_ctx",
    )(*ins)


def _mla_attn_body(has_lat, tk, sub, q_ref, *refs):
    if has_lat:
        kl_ref, vl_ref, kc_ref, vc_ref, o_ref, s_scr = refs
    else:
        kc_ref, vc_ref, o_ref, s_scr = refs
    tq = q_ref.shape[2]
    n_ctx = kc_ref.shape[2]
    low = lax.broadcasted_iota(jnp.int32, (sub, LANES), 1) < MLA_NOPE
    chains = []
    for r in range(tq // sub):
        for e in range(2):
            parts = []
            if has_lat:
                for j in range(kl_ref.shape[2] // tk):
                    parts.append((lambda e=e, j=j: kl_ref[0, e, j * tk:(j + 1) * tk, :],
                                  lambda j=j: vl_ref[0, j * tk:(j + 1) * tk, :], None, tk))
            parts.append((lambda e=e: kc_ref[0, e], lambda: vc_ref[0], None, n_ctx))
            chains.append((q_ref[0, e, r * sub:(r + 1) * sub, :], parts, None))
    o = _attend(chains, s_scr)
    for r in range(tq // sub):
        o_ref[0, r * sub:(r + 1) * sub, :] = jnp.where(low, o[2 * r], o[2 * r + 1]).astype(o_ref.dtype)


def _mla_attn(q, kl, vl, kc, vc, tq, sub):
    b, nh, l, _ = q.shape
    n_ctx = kc.shape[2]
    has_lat = kl is not None
    ins = [q]
    specs = [pl.BlockSpec((1, 2, tq, LANES), lambda bb, j, i: (bb, j, i, 0))]
    if has_lat:
        ins += [kl, vl]
        specs += [pl.BlockSpec((1, 2, l, LANES), lambda bb, j, i: (bb, j, 0, 0)),
                  pl.BlockSpec((1, l, LANES), lambda bb, j, i: (bb, 0, j))]
    ins += [kc, vc]
    specs += [pl.BlockSpec((1, 2, n_ctx, LANES), lambda bb, j, i: (bb, j, 0, 0)),
              pl.BlockSpec((1, n_ctx, LANES), lambda bb, j, i: (bb, 0, j))]
    return pl.pallas_call(
        functools.partial(_mla_attn_body, has_lat, MLA_KEY_BLOCK, sub),
        grid=(b, nh // 2, l // tq),
        in_specs=specs,
        out_specs=pl.BlockSpec((1, tq, LANES), lambda bb, j, i: (bb, i, j)),
        out_shape=jax.ShapeDtypeStruct((b, l, nh * MLA_NOPE), BF16),
        scratch_shapes=[pltpu.VMEM((2, sub, n_ctx + (l if has_lat else 0)), F32)],
        compiler_params=_cparams("parallel", "parallel", "parallel"),
        name="mla_attn_lat" if has_lat else "mla_attn_ctx",
    )(*ins)


def _lru_in_body(x_ref, gain_ref, sh_ref, sc_ref, w_ref, g_out, x_out):
    h = _modnorm(x_ref[0], gain_ref[...], sh_ref[0], sc_ref[0]).astype(BF16)
    u = _dot(h, w_ref[...])
    width = u.shape[1] // 2
    g_out[0] = jax.nn.gelu(u[:, :width], approximate=True).astype(BF16)
    x_out[0] = u[:, width:]


def _lru_in(x, gain, sh, sc, w_in, tm):
    b, l, d = x.shape
    width = w_in.shape[1] // 2
    return pl.pallas_call(
        _lru_in_body,
        grid=(b, l // tm),
        in_specs=[_rows(tm, d), _const(gain.shape), _vec(d), _vec(d), _const(w_in.shape)],
        out_specs=[_rows(tm, width), _rows(tm, width)],
        out_shape=[jax.ShapeDtypeStruct((b, l, width), BF16), jax.ShapeDtypeStruct((b, l, width), F32)],
        compiler_params=_cparams("parallel", "parallel"),
        name="lru_in",
    )(x, gain, sh, sc, w_in)


def _halo_specs(tm, l, d, halo=HALO):
    nb = tm // halo
    last = l // halo - 1
    prev = pl.BlockSpec((1, halo, d), lambda b, i: (b, jnp.maximum(i * nb - 1, 0), 0))
    nxt = pl.BlockSpec((1, halo, d), lambda b, i: (b, jnp.minimum((i + 1) * nb, last), 0))
    return prev, nxt


def _lru_coef_body(x_ref, xp_ref, xn_ref, cw_ref, gw_ref, gb_ref, lam_ref, af_out, bf_out, ar_out, br_out):
    i = pl.program_id(1)
    tm = x_ref.shape[1]
    x = x_ref[0]
    pm = (i > 0).astype(F32)
    nm = (i < pl.num_programs(1) - 1).astype(F32)
    p2 = xp_ref[0, HALO - 2:HALO - 1, :] * pm
    p1 = xp_ref[0, HALO - 1:HALO, :] * pm
    n1 = xn_ref[0, 0:1, :] * nm
    row = lax.broadcasted_iota(jnp.int32, x.shape, 0)
    xm1 = jnp.where(row == 0, p1, pltpu.roll(x, 1, 0))
    xm2 = jnp.where(row == 0, p2, jnp.where(row == 1, p1, pltpu.roll(x, 2, 0)))
    xp1 = jnp.where(row == tm - 1, n1, pltpu.roll(x, tm - 1, 0))
    cw = cw_ref[...]
    xc = cw[0:1] * xm2 + cw[1:2] * xm1 + cw[2:3] * x + cw[3:4] * xp1 + cw[4:5]
    xcb = xc.astype(BF16)
    outs = ((af_out, bf_out), (ar_out, br_out))
    for d in range(2):
        gates = []
        for k in range(2):
            g = jnp.concatenate(
                [_dot(xcb[:, n * LRU_BW:(n + 1) * LRU_BW], gw_ref[(d * 2 + k) * LRU_BLOCKS + n])
                 for n in range(LRU_BLOCKS)], axis=1)
            gates.append(jax.nn.sigmoid(g + gb_ref[d * 2 + k:d * 2 + k + 1, :]))
        r, ig = gates
        nl = -lam_ref[d:d + 1, :]
        softplus = jnp.maximum(nl, 0.0) + jnp.log1p(jnp.exp(-jnp.abs(nl)))
        log_a = (-LRU_C) * r * softplus
        a = jnp.exp(log_a)
        outs[d][0][0] = a
        outs[d][1][0] = jnp.sqrt(-jnp.tanh(log_a) * (a * a + 1.0)) * (ig * xc)


def _lru_coef(xr, cw, gw, gb, lam, tm):
    b, l, d = xr.shape
    prev, nxt = _halo_specs(tm, l, d)
    out = jax.ShapeDtypeStruct((b, l, d), F32)
    return pl.pallas_call(
        _lru_coef_body,
        grid=(b, l // tm),
        in_specs=[_rows(tm, d), prev, nxt, _const(cw.shape), _const(gw.shape), _const(gb.shape),
                  _const(lam.shape)],
        out_specs=[_rows(tm, d)] * 4,
        out_shape=[out] * 4,
        compiler_params=_cparams("parallel", "parallel"),
        name="lru_coef",
    )(xr, xr, xr, cw, gw, gb, lam)


def _scan_body(reverse, has_add, a_ref, b_ref, h0_ref, *rest):
    if has_add:
        add_ref, o_ref, he_ref, h_s, a_s, b_s = rest
    else:
        o_ref, he_ref, h_s, a_s, b_s = rest
    tm, width = a_ref.shape[1], a_ref.shape[2]
    n_groups = tm // SUBLANES

    @pl.when(pl.program_id(1) == 0)
    def _():
        h_s[...] = jnp.broadcast_to(h0_ref[0], h_s.shape)

    a = a_ref[0].reshape(n_groups, SUBLANES, width)
    b = b_ref[0].reshape(n_groups, SUBLANES, width)
    in_group = lax.broadcasted_iota(jnp.int32, a.shape, 1)
    for k in (1, 2, 4):
        if reverse:
            a_sh, b_sh = pltpu.roll(a, SUBLANES - k, 1), pltpu.roll(b, SUBLANES - k, 1)
            valid = in_group < SUBLANES - k
        else:
            a_sh, b_sh = pltpu.roll(a, k, 1), pltpu.roll(b, k, 1)
            valid = in_group >= k
        b = jnp.where(valid, b + a * b_sh, b)
        a = jnp.where(valid, a * a_sh, a)
    a_s[...] = a.reshape(tm, width)
    b_s[...] = b.reshape(tm, width)

    def group(j, carry):
        g = n_groups - 1 - j if reverse else j
        rows = pl.ds(pl.multiple_of(g * SUBLANES, SUBLANES), SUBLANES)
        h = a_s[rows, :] * carry + b_s[rows, :]
        o_ref[0, rows, :] = h + add_ref[0, rows, :] if has_add else h
        last = h[0:1, :] if reverse else h[SUBLANES - 1:SUBLANES, :]
        return jnp.broadcast_to(last, carry.shape)

    carry = lax.fori_loop(0, n_groups, group, h_s[...], unroll=8)
    h_s[...] = carry
    he_ref[0] = carry[0:1, :]


def _scan(a, b, h0, add, reverse, tm):
    nb, t, width = a.shape
    nt = t // tm
    imap = (lambda bb, i: (bb, nt - 1 - i, 0)) if reverse else (lambda bb, i: (bb, i, 0))
    blk = pl.BlockSpec((1, tm, width), imap)
    hspec = pl.BlockSpec((1, 1, width), lambda bb, i: (bb, 0, 0))
    ins = [a, b, h0] + ([add] if add is not None else [])
    specs = [blk, blk, hspec] + ([blk] if add is not None else [])
    return pl.pallas_call(
        functools.partial(_scan_body, reverse, add is not None),
        grid=(nb, nt),
        in_specs=specs,
        out_specs=[blk, hspec],
        out_shape=[jax.ShapeDtypeStruct((nb, t, width), F32), jax.ShapeDtypeStruct((nb, 1, width), F32)],
        scratch_shapes=[pltpu.VMEM((SUBLANES, width), F32), pltpu.VMEM((tm, width), F32),
                        pltpu.VMEM((tm, width), F32)],
        compiler_params=_cparams("parallel", "arbitrary"),
        name="lru_scan_rev" if reverse else "lru_scan_fwd",
    )(*ins)


def _block_tail_body(has_r, x_ref, xp_ref, xn_ref, o_ref, op_ref, on_ref, *rest):
    if has_r:
        r_ref, rp_ref, rn_ref = rest[:3]
        rest = rest[3:]
    (wo_ref, g1_ref, gain_ref, sh_ref, sc_ref, g2_ref, wu_ref, cw_ref, wd_ref, y_ref, x1_s, h_s, act_s) = rest
    i = pl.program_id(1)
    tm = x_ref.shape[1]
    d_ff = wd_ref.shape[0]
    gain, sh, sc, g1 = gain_ref[...], sh_ref[0], sc_ref[0], g1_ref[0]

    def mixed(o, r):
        return o if r is None else (o.astype(F32) * r).astype(BF16)

    wo = wo_ref[...]
    x1 = x_ref[0] + g1 * _dot(mixed(o_ref[0], r_ref[0] if has_r else None), wo)
    x1_s[...] = x1
    h_s[0:tm, :] = _modnorm(x1, gain, sh, sc).astype(BF16)
    o_halo = jnp.concatenate([op_ref[0], on_ref[0]], axis=0)
    r_halo = jnp.concatenate([rp_ref[0], rn_ref[0]], axis=0) if has_r else None
    x_halo = jnp.concatenate([xp_ref[0], xn_ref[0]], axis=0)
    x1_halo = x_halo + g1 * _dot(mixed(o_halo, r_halo), wo)
    h_s[tm:tm + 2 * FFN_HALO, :] = _modnorm(x1_halo, gain, sh, sc).astype(BF16)
    pm = (i > 0).astype(F32)
    nm = (i < pl.num_programs(1) - 1).astype(F32)

    def up(c):
        lo = c * FF_CHUNK
        return (_dot(h_s[...], wu_ref[:, lo:lo + FF_CHUNK]),
                _dot(h_s[0:tm, :], wu_ref[:, d_ff + lo:d_ff + lo + FF_CHUNK]))

    nxt = up(0)
    for c in range(d_ff // FF_CHUNK):
        ge, val = nxt
        if (c + 1) * FF_CHUNK < d_ff:
            nxt = up(c + 1)
        lo = c * FF_CHUNK
        g = ge[0:tm]
        gp = ge[tm + FFN_HALO - 1:tm + FFN_HALO] * pm
        gn = ge[tm + FFN_HALO:tm + FFN_HALO + 1] * nm
        row = lax.broadcasted_iota(jnp.int32, g.shape, 0)
        gd = jnp.where(row == 0, gp, pltpu.roll(g, 1, 0))
        gu = jnp.where(row == tm - 1, gn, pltpu.roll(g, tm - 1, 0))
        cw = cw_ref[:, lo:lo + FF_CHUNK]
        y = cw[0:1] * gd + cw[1:2] * g + cw[2:3] * gu + cw[3:4]
        act_s[:, lo:lo + FF_CHUNK] = (y * jax.nn.sigmoid(y) * val).astype(BF16)
    y_ref[0] = x1_s[...] + g2_ref[0] * _dot(act_s[...], wd_ref[...])


def _block_tail(x, o, r, w_o, g1, gain, sh, sc, g2, w, tm):
    b, l, d = x.shape
    dk = o.shape[2]
    wu, cw, wd = w
    xs = [_rows(tm, d), *_halo_specs(tm, l, d, FFN_HALO)]
    os_ = [_rows(tm, dk), *_halo_specs(tm, l, dk, FFN_HALO)]
    ins = [x, x, x, o, o, o] + ([r, r, r] if r is not None else [])
    specs = xs + os_ + (os_ if r is not None else [])
    ins += [w_o, g1, gain, sh, sc, g2, wu, cw, wd]
    specs += [_const(w_o.shape), _vec(d), _const(gain.shape), _vec(d), _vec(d), _vec(d),
              _const(wu.shape), _const(cw.shape), _const(wd.shape)]
    return pl.pallas_call(
        functools.partial(_block_tail_body, r is not None),
        grid=(b, l // tm),
        in_specs=specs,
        out_specs=_rows(tm, d),
        out_shape=jax.ShapeDtypeStruct((b, l, d), F32),
        scratch_shapes=[pltpu.VMEM((tm, d), F32), pltpu.VMEM((tm + 2 * FFN_HALO, d), BF16),
                        pltpu.VMEM((tm, wd.shape[0]), BF16)],
        compiler_params=_cparams("parallel", "parallel"),
        name="block_tail",
    )(*ins)


def _rope_cs(n_tokens, rot_dim):
    pos = jnp.arange(n_tokens)
    row = (pos // GRID_W).astype(F32)
    col = (pos % GRID_W).astype(F32)
    n_freq = rot_dim // 4
    inv = ROPE_BASE ** (-jnp.arange(n_freq, dtype=F32) / n_freq)
    ar, ac = row[:, None] * inv, col[:, None] * inv
    cos = jnp.concatenate([jnp.cos(ar), jnp.cos(ar), jnp.cos(ac), jnp.cos(ac)], axis=1)
    sin = jnp.concatenate([-jnp.sin(ar), jnp.sin(ar), -jnp.sin(ac), jnp.sin(ac)], axis=1)
    return cos, sin


def _pad_rows(a, rows):
    return jnp.pad(a, ((0, rows - a.shape[0]), (0, 0)))


def _pair_swap(n, half):
    i = jnp.arange(n)
    return jnp.where((i & half) == 0, i + half, i - half)


def _perm_matrix(half):
    return (jnp.arange(MXU_COLS)[:, None] == _pair_swap(MXU_COLS, half)[None, :]).astype(BF16)


def _group_ones(size):
    g = jnp.arange(MXU_COLS) // size
    return (g[:, None] == g[None, :]).astype(BF16)


def _gain_tables(gain, cos, sin, half):
    return gain[None, :] * cos, gain[_pair_swap(LANES, half)][None, :] * sin, gain[None, :]


def kernel(x, c, ctx, c_ctx, norm1, norm2, mod_w, mod_b, swa_w_qkv, swa_q_gain, swa_k_gain, swa_sink, swa_w_o, mla_w_down, mla_q_lora_gain, mla_w_uq, mla_kv_lora_gain, mla_w_uk, mla_w_uv, mla_q_gain, mla_k_gain, mla_w_o, lru_w_in, lru_conv_w, lru_conv_b, lru_gate_w, lru_gate_b, lru_lam, lru_w_out, ffn_w_up, ffn_conv_w, ffn_conv_b, ffn_w_down):
    bsz, seq, d = x.shape
    n_ctx = ctx.shape[1]
    depth = norm1.shape[0]
    tm_lat, tm_ctx = ROW_TILE, n_ctx

    cond = _pad_rows(jnp.concatenate([c, c_ctx[None, :]], axis=0), 16)
    mods = _mod_all(cond, mod_w, mod_b)

    cos_s, sin_s = _rope_cs(seq, HEAD_DIM)
    cos_s, sin_s = jnp.tile(cos_s, (1, 2)), jnp.tile(sin_s, (1, 2))
    cos_m, sin_m = _rope_cs(seq, MLA_ROPE)
    one = jnp.ones((seq, MLA_NOPE), F32)
    zero = jnp.zeros((seq, MLA_NOPE), F32)
    cos_m = jnp.concatenate([one, cos_m, one[:, :LANES - MLA_QK]], axis=1)
    sin_m = jnp.concatenate([zero, sin_m, zero[:, :LANES - MLA_QK]], axis=1)
    ones64, ones128 = _group_ones(HEAD_DIM), _group_ones(LANES)

    for layer in range(depth):
        kind, idx = layer % 3, layer // 3
        with_ctx_out = layer < depth - 1
        m = mods[layer].reshape(16, 6, d)
        ml = [m[:bsz, j][:, None, :] for j in range(6)]
        mc = [jnp.broadcast_to(m[bsz, j][None, None, :], (bsz, 1, d)) for j in range(6)]
        n1 = norm1[layer][None, :]
        n2 = norm2[layer][None, :]
        r_lat = r_ctx = None
        if kind == 0:
            wqkv = swa_w_qkv[idx]
            nq = SWA_HEADS * HEAD_DIM
            nkv = SWA_KV_HEADS * HEAD_DIM

            def dup(w):
                w = w.reshape(d, SWA_KV_HEADS, 1, HEAD_DIM)
                return jnp.broadcast_to(w, (d, SWA_KV_HEADS, 2, HEAD_DIM)).reshape(d, 2 * nkv).astype(BF16)

            w = (wqkv[:, :nq].astype(BF16), dup(wqkv[:, nq:nq + nkv]), dup(wqkv[:, nq + nkv:]),
                 ones64, ones128, _perm_matrix(HEAD_DIM // 4))
            cq, sq, gq = _gain_tables(jnp.tile(swa_q_gain[idx], 2) * LOG2E, cos_s, sin_s, HEAD_DIM // 4)
            ck, sk, gk = _gain_tables(jnp.tile(swa_k_gain[idx], 2) * (2 * HEAD_DIM) ** 0.5, cos_s, sin_s,
                                      HEAD_DIM // 4)
            sink = swa_sink[idx] * LOG2E
            qc, kc, vc = _swa_proj(ctx, n1, mc[0], mc[1], w, (gq, gk), False, tm_ctx)
            ql, kl, vl = _swa_proj(x, n1, ml[0], ml[1], w, (cq, ck, sq, sk), True, tm_lat)
            o_lat = _swa_attn(sink, ql, kl, vl, kc, vc, ATTN_Q_TILE, ATTN_ROWS)
            o_ctx = _swa_attn(sink, qc, None, None, kc, vc, n_ctx, n_ctx) if with_ctx_out else None
            w_o = swa_w_o[idx].astype(BF16)
        elif kind == 1:
            wdn = mla_w_down[idx]
            zc = jnp.zeros((d, MLA_NOPE), F32)
            wd_pad = jnp.concatenate([wdn[:, :Q_LORA + KV_LORA], zc, wdn[:, Q_LORA + KV_LORA:],
                                      zc[:, :LANES - MLA_QK]], axis=1).astype(BF16)
            wuq = jnp.pad(mla_w_uq[idx].reshape(Q_LORA, MLA_HEADS, MLA_QK),
                          ((0, 0), (0, 0), (0, LANES - MLA_QK))).reshape(Q_LORA, MLA_HEADS * LANES).astype(BF16)
            wuk = jnp.pad(mla_w_uk[idx].reshape(KV_LORA, MLA_HEADS, MLA_NOPE),
                          ((0, 0), (0, 0), (0, LANES - MLA_NOPE))).reshape(KV_LORA, MLA_HEADS * LANES).astype(BF16)
            root = MLA_QK ** 0.5
            gq = jnp.pad(mla_q_gain[idx] * LOG2E, (0, LANES - MLA_QK))
            gk_nope = jnp.pad(mla_k_gain[idx][:MLA_NOPE] * root, (0, LANES - MLA_NOPE))[None, :]
            gk_rope = jnp.pad(mla_k_gain[idx][MLA_NOPE:] * root, (MLA_NOPE, LANES - MLA_QK))
            cq, sq, gq = _gain_tables(gq, cos_m, sin_m, MLA_ROPE // 4)
            ck, sk, gk = _gain_tables(gk_rope, cos_m, sin_m, MLA_ROPE // 4)
            w = (wd_pad, mla_q_lora_gain[idx][None, :], wuq, mla_kv_lora_gain[idx][None, :], wuk,
                 mla_w_uv[idx].astype(BF16), ones128, _perm_matrix(MLA_ROPE // 4), gk_nope)
            qc, kc, vc = _mla_proj(ctx, n1, mc[0], mc[1], w, (gq, gk), False, tm_ctx)
            ql, kl, vl = _mla_proj(x, n1, ml[0], ml[1], w, (cq, ck, sq, sk), True, tm_lat)
            o_lat = _mla_attn(ql, kl, vl, kc, vc, ATTN_Q_TILE, ATTN_ROWS)
            o_ctx = _mla_attn(qc, None, None, kc, vc, n_ctx, n_ctx) if with_ctx_out else None
            w_o = mla_w_o[idx].astype(BF16)
        else:
            w_in = lru_w_in[idx].astype(BF16)
            cw = _pad_rows(jnp.concatenate([lru_conv_w[idx], lru_conv_b[idx][None, :]], axis=0), 8)
            gw = lru_gate_w[idx].reshape(2 * 2 * LRU_BLOCKS, LRU_BW, LRU_BW).astype(BF16)
            gb = _pad_rows(lru_gate_b[idx].reshape(4, d), 8)
            lam = _pad_rows(lru_lam[idx], 8)
            g_ctx, xr_ctx = _lru_in(ctx, n1, mc[0], mc[1], w_in, tm_ctx)
            g_lat, xr_lat = _lru_in(x, n1, ml[0], ml[1], w_in, tm_lat)
            afc, bfc, arc, brc = _lru_coef(xr_ctx, cw, gw, gb, lam, tm_ctx)
            afl, bfl, arl, brl = _lru_coef(xr_lat, cw, gw, gb, lam, tm_lat)
            h0 = jnp.zeros((bsz, 1, d), F32)
            sfc, hfc = _scan(afc, bfc, h0, None, False, tm_ctx)
            r_ctx, hrc = _scan(arc, brc, h0, sfc, True, tm_ctx)
            sfl, _ = _scan(afl, bfl, hfc, None, False, tm_lat)
            r_lat, _ = _scan(arl, brl, hrc, sfl, True, tm_lat)
            o_lat, o_ctx = g_lat, (g_ctx if with_ctx_out else None)
            w_o = lru_w_out[idx].astype(BF16)

        cw_ffn = _pad_rows(jnp.concatenate([ffn_conv_w[layer], ffn_conv_b[layer][None, :]], axis=0), 8)
        wf = (ffn_w_up[layer].astype(BF16), cw_ffn, ffn_w_down[layer].astype(BF16))

        x = _block_tail(x, o_lat, r_lat, w_o, ml[2], n2, ml[3], ml[4], ml[5], wf, tm_lat)
        if with_ctx_out:
            ctx = _block_tail(ctx, o_ctx, r_ctx, w_o, mc[2], n2, mc[3], mc[4], mc[5], wf, tm_ctx)
    return x
```

```python
import functools

import jax
import jax.numpy as jnp
from jax import lax
from jax.experimental import pallas as pl
from jax.experimental.pallas import tpu as pltpu

F32 = jnp.float32
BF16 = jnp.bfloat16

GRID_W = 64
HEAD_DIM = 64
SWA_HEADS = 16
SWA_KV_HEADS = 4
WINDOW = 128
MLA_HEADS = 16
MLA_NOPE = 64
MLA_ROPE = 32
MLA_QK = MLA_NOPE + MLA_ROPE
Q_LORA = 384
KV_LORA = 256
LRU_BLOCKS = 4
LRU_BW = 256
LRU_C = 8.0
D_FF = 2816
ROPE_BASE = 10000.0
EPS = 1e-6
NEG_INF = -1e30
LOG2E = 1.4426950408889634

LANES = 128
MXU_COLS = 256
FF_CHUNK = MXU_COLS
SUBLANES = 8
HALO = SUBLANES
FFN_HALO = 16
MLA_KEY_BLOCK = 512
ROW_TILE = 512
ATTN_ROWS = MXU_COLS
ATTN_Q_TILE = 1024
VMEM_LIMIT = 56 * 1024 * 1024


def _cparams(*sem):
    return pltpu.CompilerParams(dimension_semantics=sem, vmem_limit_bytes=VMEM_LIMIT)


def _const(shape):
    nd = len(shape)
    return pl.BlockSpec(shape, lambda *_: (0,) * nd, pipeline_mode=pl.Buffered(1))


def _rows(tm, d):
    return pl.BlockSpec((1, tm, d), lambda b, i: (b, i, 0))


def _vec(d):
    return pl.BlockSpec((1, 1, d), lambda b, i: (b, 0, 0))


def _dot(a, b):
    return jnp.dot(a, b, preferred_element_type=F32)


def _dot_t(a, b):
    return lax.dot_general(a, b, (((1,), (1,)), ((), ())), preferred_element_type=F32)


def _modnorm(x, gain, shift, scale):
    ms = jnp.mean(x * x, axis=-1, keepdims=True)
    y = x * lax.rsqrt(ms + EPS) * gain
    return y * (1.0 + scale) + shift


def _chunk_dot(x, m):
    w = m.shape[0]
    return jnp.concatenate([_dot(x[:, j * w:(j + 1) * w], m) for j in range(x.shape[1] // w)], axis=1)


def _lane_tile(t, width):
    return jnp.concatenate([t] * (width // t.shape[1]), axis=1)


def _norm_rope(x, group_ones, perm, cos, sin, eps_sum):
    ssq = _chunk_dot((x * x).astype(BF16), group_ones)
    y = x * _lane_tile(cos, x.shape[1])
    if sin is not None:
        y = y + _chunk_dot(x.astype(BF16), perm) * _lane_tile(sin, x.shape[1])
    return y * lax.rsqrt(ssq + eps_sum)


def _mod_body(c_ref, w_ref, b_ref, o_ref):
    c = c_ref[...]
    cs = (c * jax.nn.sigmoid(c)).astype(BF16)
    o_ref[0] = _dot(cs, w_ref[0].astype(BF16)) + b_ref[0]


def _mod_all(cond, mod_w, mod_b):
    depth, d, n = mod_w.shape
    tn = 1536
    rows = cond.shape[0]
    return pl.pallas_call(
        _mod_body,
        grid=(depth, n // tn),
        in_specs=[pl.BlockSpec((rows, d), lambda l, j: (0, 0)),
                  pl.BlockSpec((1, d, tn), lambda l, j: (l, 0, j)),
                  pl.BlockSpec((1, 1, tn), lambda l, j: (l, 0, j))],
        out_specs=pl.BlockSpec((1, rows, tn), lambda l, j: (l, 0, j)),
        out_shape=jax.ShapeDtypeStruct((depth, rows, n), F32),
        compiler_params=_cparams("arbitrary", "arbitrary"),
        name="mod",
    )(cond, mod_w, mod_b.reshape(depth, 1, n))


def _swa_proj_body(rope, x_ref, gain_ref, sh_ref, sc_ref, wq_ref, wk_ref, wv_ref, g64_ref, g128_ref,
                   perm_ref, cq_ref, ck_ref, *rest):
    if rope:
        sq_ref, sk_ref, q_out, k_out, v_out = rest
        sq, sk = sq_ref[...], sk_ref[...]
    else:
        q_out, k_out, v_out = rest
        sq = sk = None
    h = _modnorm(x_ref[0], gain_ref[...], sh_ref[0], sc_ref[0]).astype(BF16)
    q = _dot(h, wq_ref[...])
    k = _dot(h, wk_ref[...])
    v_out[0] = _dot(h, wv_ref[...]).astype(BF16)
    perm = perm_ref[...]
    q_out[0] = _norm_rope(q, g64_ref[...], perm, cq_ref[...], sq, HEAD_DIM * EPS).astype(BF16)
    k_out[0] = _norm_rope(k, g128_ref[...], perm, ck_ref[...], sk, 2 * HEAD_DIM * EPS).astype(BF16)


def _table_specs(tables, rope, tm):
    if rope:
        return [pl.BlockSpec((tm, LANES), lambda bb, i: (i, 0))] * len(tables)
    return [_const(t.shape) for t in tables]


def _swa_proj(x, gain, sh, sc, w, tables, rope, tm):
    b, l, d = x.shape
    wq, wk, wv = w[:3]
    ins = [x, gain, sh, sc] + list(w) + list(tables)
    specs = ([_rows(tm, d), _const(gain.shape), _vec(d), _vec(d)] + [_const(a.shape) for a in w]
             + _table_specs(tables, rope, tm))
    nq, nk = wq.shape[1], wk.shape[1]
    return pl.pallas_call(
        functools.partial(_swa_proj_body, rope),
        grid=(b, l // tm),
        in_specs=specs,
        out_specs=[_rows(tm, nq), _rows(tm, nk), _rows(tm, nk)],
        out_shape=[jax.ShapeDtypeStruct((b, l, nq), BF16), jax.ShapeDtypeStruct((b, l, nk), BF16),
                   jax.ShapeDtypeStruct((b, l, nk), BF16)],
        compiler_params=_cparams("parallel", "parallel"),
        name="swa_proj_lat" if rope else "swa_proj_ctx",
    )(*ins)


def _fold(x, op):
    acc = x[:, :LANES]
    for t in range(1, x.shape[1] // LANES):
        acc = op(acc, x[:, t * LANES:(t + 1) * LANES])
    return acc


def _attend(chains, s_scr):
    def phase1(ci):
        q, parts, _ = chains[ci]
        st = {"mx": None}

        def make(p, off):
            def step():
                k_thunk, _, bias, nk = parts[p]
                s = _dot_t(q, k_thunk())
                if bias is not None:
                    s = s + bias
                s_scr[ci % 2, :, off:off + nk] = s
                f = _fold(s, jnp.maximum)
                st["mx"] = f if st["mx"] is None else jnp.maximum(st["mx"], f)
            return step

        steps, off = [], 0
        for p, part in enumerate(parts):
            steps.append(make(p, off))
            off += part[3]
        return st, steps

    def phase2(ci, st):
        _, parts, sink = chains[ci]
        out = {"acc": [None, None]}

        def make(p, off):
            def step():
                if p == 0:
                    m = jnp.max(st["mx"], axis=-1, keepdims=True)
                    out["m"] = m if sink is None else jnp.maximum(m, sink)
                nk = parts[p][3]
                pr = jnp.exp2(s_scr[ci % 2, :, off:off + nk] - out["m"])
                v = parts[p][1]()
                pv = _dot(pr.astype(BF16), jnp.concatenate([v, jnp.ones_like(v)], axis=1))
                out["acc"][p % 2] = pv if out["acc"][p % 2] is None else out["acc"][p % 2] + pv
            return step

        def finish():
            acc = out["acc"][0] if out["acc"][1] is None else out["acc"][0] + out["acc"][1]
            den = acc[:, LANES:]
            if sink is not None:
                den = den + jnp.exp2(sink - out["m"])
            return acc[:, :LANES] / den

        steps, off = [], 0
        for p, part in enumerate(parts):
            steps.append(make(p, off))
            off += part[3]
        return steps, finish

    results = []
    pending = None
    for ci in range(len(chains) + 1):
        st, steps1 = phase1(ci) if ci < len(chains) else (None, [])
        steps2 = pending[0] if pending else []
        for k in range(max(len(steps1), len(steps2))):
            if k < len(steps1):
                steps1[k]()
            if k < len(steps2):
                steps2[k]()
        if pending:
            results.append(pending[1]())
        pending = phase2(ci, st) if ci < len(chains) else None
    return results


def _swa_attn_body(has_win, tq, sub, seq, sink_ref, q_ref, *refs):
    if has_win:
        kd_ref, vd_ref, kc_ref, vc_ref, o_ref, s_scr = refs
    else:
        kc_ref, vc_ref, o_ref, s_scr = refs
    hk = pl.program_id(1)
    n_ctx = kc_ref.shape[1]
    span = sub + 2 * WINDOW
    low = lax.broadcasted_iota(jnp.int32, (sub, LANES), 1) < HEAD_DIM
    chains = []
    for r in range(tq // sub):
        if has_win:
            s0 = pl.program_id(2) * tq + r * sub
            ws = pl.multiple_of(jnp.clip(s0 - WINDOW, 0, seq - span), LANES)
            qpos = s0 + lax.broadcasted_iota(jnp.int32, (sub, span), 0)
            kpos = ws + lax.broadcasted_iota(jnp.int32, (sub, span), 1)
            bias = jnp.where(jnp.abs(qpos - kpos) <= WINDOW, 0.0, NEG_INF).astype(F32)
        for g in range(4):
            qp = q_ref[0, r * sub:(r + 1) * sub, (g // 2) * LANES:(g // 2 + 1) * LANES]
            qm = jnp.where(low if g % 2 == 0 else jnp.logical_not(low), qp, jnp.zeros_like(qp))
            parts = []
            if has_win:
                parts.append((lambda ws=ws: kd_ref[0, pl.ds(ws, span), :],
                              lambda ws=ws: vd_ref[0, pl.ds(ws, span), :], bias, span))
            parts.append((lambda: kc_ref[0], lambda: vc_ref[0], None, n_ctx))
            chains.append((qm, parts, sink_ref[hk * 4 + g]))
    o = _attend(chains, s_scr)
    for r in range(tq // sub):
        o_ref[0, r * sub:(r + 1) * sub, :] = jnp.concatenate(
            [jnp.where(low, o[4 * r], o[4 * r + 1]), jnp.where(low, o[4 * r + 2], o[4 * r + 3])],
            axis=1).astype(o_ref.dtype)


def _swa_attn(sink, q, kd, vd, kcd, vcd, tq, sub):
    b, l, dq = q.shape
    n_ctx = kcd.shape[1]
    has_win = kd is not None
    gw = dq // SWA_KV_HEADS
    ins = [sink, q]
    specs = [pl.BlockSpec(memory_space=pltpu.SMEM),
             pl.BlockSpec((1, tq, gw), lambda bb, h, i: (bb, i, h))]
    if has_win:
        ins += [kd, vd]
        specs += [pl.BlockSpec((1, l, LANES), lambda bb, h, i: (bb, 0, h))] * 2
    ins += [kcd, vcd]
    specs += [pl.BlockSpec((1, n_ctx, LANES), lambda bb, h, i: (bb, 0, h))] * 2
    return pl.pallas_call(
        functools.partial(_swa_attn_body, has_win, tq, sub, l),
        grid=(b, SWA_KV_HEADS, l // tq),
        in_specs=specs,
        out_specs=pl.BlockSpec((1, tq, gw), lambda bb, h, i: (bb, i, h)),
        out_shape=jax.ShapeDtypeStruct((b, l, dq), BF16),
        scratch_shapes=[pltpu.VMEM((2, sub, n_ctx + (sub + 2 * WINDOW if has_win else 0)), F32)],
        compiler_params=_cparams("parallel", "parallel", "parallel"),
        name="swa_attn_lat" if has_win else "swa_attn_ctx",
    )(*ins)


def _mla_proj_body(rope, x_ref, gain_ref, sh_ref, sc_ref, wd_ref, gql_ref, wuq_ref, gkvl_ref,
                   wuk_ref, wuv_ref, g128_ref, perm_ref, gkn_ref, cq_ref, ck_ref, *rest):
    if rope:
        sq_ref, sk_ref, q_out, k_out, v_out = rest
        sq, sk = sq_ref[...], sk_ref[...]
    else:
        q_out, k_out, v_out = rest
        sq = sk = None
    h = _modnorm(x_ref[0], gain_ref[...], sh_ref[0], sc_ref[0]).astype(BF16)
    t = _dot(h, wd_ref[...])
    cq = t[:, :Q_LORA]
    ckv = t[:, Q_LORA:Q_LORA + KV_LORA]
    kr = t[:, Q_LORA + KV_LORA:]
    cq = (cq * lax.rsqrt(jnp.mean(cq * cq, axis=-1, keepdims=True) + EPS) * gql_ref[...]).astype(BF16)
    ckv = (ckv * lax.rsqrt(jnp.mean(ckv * ckv, axis=-1, keepdims=True) + EPS) * gkvl_ref[...]).astype(BF16)
    qa = _dot(cq, wuq_ref[...])
    ka = _dot(ckv, wuk_ref[...])
    v_out[0] = _dot(ckv, wuv_ref[...]).astype(BF16)
    g128, perm = g128_ref[...], perm_ref[...]
    q = _norm_rope(qa, g128, perm, cq_ref[...], sq, MLA_QK * EPS).astype(BF16)
    kr_rot = kr * ck_ref[...]
    if rope:
        kr_rot = kr_rot + _dot(kr.astype(BF16), perm[:LANES, :LANES]) * sk
    width = ka.shape[1]
    kraw = ka + _lane_tile(kr, width)
    ssq = _chunk_dot((kraw * kraw).astype(BF16), g128)
    k = ((ka * _lane_tile(gkn_ref[...], width) + _lane_tile(kr_rot, width))
         * lax.rsqrt(ssq + MLA_QK * EPS)).astype(BF16)
    for hh in range(MLA_HEADS):
        q_out[0, hh] = q[:, hh * LANES:(hh + 1) * LANES]
        k_out[0, hh] = k[:, hh * LANES:(hh + 1) * LANES]


def _mla_proj(x, gain, sh, sc, w, tables, rope, tm):
    b, l, d = x.shape
    ins = [x, gain, sh, sc] + list(w) + list(tables)
    specs = ([_rows(tm, d), _const(gain.shape), _vec(d), _vec(d)] + [_const(a.shape) for a in w]
             + _table_specs(tables, rope, tm))
    hspec = pl.BlockSpec((1, MLA_HEADS, tm, LANES), lambda bb, i: (bb, 0, i, 0))
    hshape = jax.ShapeDtypeStruct((b, MLA_HEADS, l, LANES), BF16)
    nv = w[5].shape[1]
    return pl.pallas_call(
        functools.partial(_mla_proj_body, rope),
        grid=(b, l // tm),
        in_specs=specs,
        out_specs=[hspec, hspec, _rows(tm, nv)],
        out_shape=[hshape, hshape, jax.ShapeDtypeStruct((b, l, nv), BF16)],
        compiler_params=_cparams("parallel", "parallel"),
        name="mla_proj_lat" if rope else "mla_proj_ctx",
    )(*ins)


def _mla_attn_body(has_lat, tk, sub, q_ref, *refs):
    if has_lat:
        kl_ref, vl_ref, kc_ref, vc_ref, o_ref, s_scr = refs
    else:
        kc_ref, vc_ref, o_ref, s_scr = refs
    tq = q_ref.shape[2]
    n_ctx = kc_ref.shape[2]
    low = lax.broadcasted_iota(jnp.int32, (sub, LANES), 1) < MLA_NOPE
    chains = []
    for r in range(tq // sub):
        for e in range(2):
            parts = []
            if has_lat:
                for j in range(kl_ref.shape[2] // tk):
                    parts.append((lambda e=e, j=j: kl_ref[0, e, j * tk:(j + 1) * tk, :],
                                  lambda j=j: vl_ref[0, j * tk:(j + 1) * tk, :], None, tk))
            parts.append((lambda e=e: kc_ref[0, e], lambda: vc_ref[0], None, n_ctx))
            chains.append((q_ref[0, e, r * sub:(r + 1) * sub, :], parts, None))
    o = _attend(chains, s_scr)
    for r in range(tq // sub):
        o_ref[0, r * sub:(r + 1) * sub, :] = jnp.where(low, o[2 * r], o[2 * r + 1]).astype(o_ref.dtype)


def _mla_attn(q, kl, vl, kc, vc, tq, sub):
    b, nh, l, _ = q.shape
    n_ctx = kc.shape[2]
    has_lat = kl is not None
    ins = [q]
    specs = [pl.BlockSpec((1, 2, tq, LANES), lambda bb, j, i: (bb, j, i, 0))]
    if has_lat:
        ins += [kl, vl]
        specs += [pl.BlockSpec((1, 2, l, LANES), lambda bb, j, i: (bb, j, 0, 0)),
                  pl.BlockSpec((1, l, LANES), lambda bb, j, i: (bb, 0, j))]
    ins += [kc, vc]
    specs += [pl.BlockSpec((1, 2, n_ctx, LANES), lambda bb, j, i: (bb, j, 0, 0)),
              pl.BlockSpec((1, n_ctx, LANES), lambda bb, j, i: (bb, 0, j))]
    return pl.pallas_call(
        functools.partial(_mla_attn_body, has_lat, MLA_KEY_BLOCK, sub),
        grid=(b, nh // 2, l // tq),
        in_specs=specs,
        out_specs=pl.BlockSpec((1, tq, LANES), lambda bb, j, i: (bb, i, j)),
        out_shape=jax.ShapeDtypeStruct((b, l, nh * MLA_NOPE), BF16),
        scratch_shapes=[pltpu.VMEM((2, sub, n_ctx + (l if has_lat else 0)), F32)],
        compiler_params=_cparams("parallel", "parallel", "parallel"),
        name="mla_attn_lat" if has_lat else "mla_attn_ctx",
    )(*ins)


def _lru_in_body(x_ref, gain_ref, sh_ref, sc_ref, w_ref, g_out, x_out):
    h = _modnorm(x_ref[0], gain_ref[...], sh_ref[0], sc_ref[0]).astype(BF16)
    u = _dot(h, w_ref[...])
    width = u.shape[1] // 2
    g_out[0] = jax.nn.gelu(u[:, :width], approximate=True).astype(BF16)
    x_out[0] = u[:, width:]


def _lru_in(x, gain, sh, sc, w_in, tm):
    b, l, d = x.shape
    width = w_in.shape[1] // 2
    return pl.pallas_call(
        _lru_in_body,
        grid=(b, l // tm),
        in_specs=[_rows(tm, d), _const(gain.shape), _vec(d), _vec(d), _const(w_in.shape)],
        out_specs=[_rows(tm, width), _rows(tm, width)],
        out_shape=[jax.ShapeDtypeStruct((b, l, width), BF16), jax.ShapeDtypeStruct((b, l, width), F32)],
        compiler_params=_cparams("parallel", "parallel"),
        name="lru_in",
    )(x, gain, sh, sc, w_in)


def _halo_specs(tm, l, d, halo=HALO):
    nb = tm // halo
    last = l // halo - 1
    prev = pl.BlockSpec((1, halo, d), lambda b, i: (b, jnp.maximum(i * nb - 1, 0), 0))
    nxt = pl.BlockSpec((1, halo, d), lambda b, i: (b, jnp.minimum((i + 1) * nb, last), 0))
    return prev, nxt


def _lru_coef_body(x_ref, xp_ref, xn_ref, cw_ref, gw_ref, gb_ref, lam_ref, af_out, bf_out, ar_out, br_out):
    i = pl.program_id(1)
    tm = x_ref.shape[1]
    x = x_ref[0]
    pm = (i > 0).astype(F32)
    nm = (i < pl.num_programs(1) - 1).astype(F32)
    p2 = xp_ref[0, HALO - 2:HALO - 1, :] * pm
    p1 = xp_ref[0, HALO - 1:HALO, :] * pm
    n1 = xn_ref[0, 0:1, :] * nm
    row = lax.broadcasted_iota(jnp.int32, x.shape, 0)
    xm1 = jnp.where(row == 0, p1, pltpu.roll(x, 1, 0))
    xm2 = jnp.where(row == 0, p2, jnp.where(row == 1, p1, pltpu.roll(x, 2, 0)))
    xp1 = jnp.where(row == tm - 1, n1, pltpu.roll(x, tm - 1, 0))
    cw = cw_ref[...]
    xc = cw[0:1] * xm2 + cw[1:2] * xm1 + cw[2:3] * x + cw[3:4] * xp1 + cw[4:5]
    xcb = xc.astype(BF16)
    outs = ((af_out, bf_out), (ar_out, br_out))
    for d in range(2):
        gates = []
        for k in range(2):
            g = jnp.concatenate(
                [_dot(xcb[:, n * LRU_BW:(n + 1) * LRU_BW], gw_ref[(d * 2 + k) * LRU_BLOCKS + n])
                 for n in range(LRU_BLOCKS)], axis=1)
            gates.append(jax.nn.sigmoid(g + gb_ref[d * 2 + k:d * 2 + k + 1, :]))
        r, ig = gates
        nl = -lam_ref[d:d + 1, :]
        softplus = jnp.maximum(nl, 0.0) + jnp.log1p(jnp.exp(-jnp.abs(nl)))
        log_a = (-LRU_C) * r * softplus
        a = jnp.exp(log_a)
        outs[d][0][0] = a
        outs[d][1][0] = jnp.sqrt(-jnp.tanh(log_a) * (a * a + 1.0)) * (ig * xc)


def _lru_coef(xr, cw, gw, gb, lam, tm):
    b, l, d = xr.shape
    prev, nxt = _halo_specs(tm, l, d)
    out = jax.ShapeDtypeStruct((b, l, d), F32)
    return pl.pallas_call(
        _lru_coef_body,
        grid=(b, l // tm),
        in_specs=[_rows(tm, d), prev, nxt, _const(cw.shape), _const(gw.shape), _const(gb.shape),
                  _const(lam.shape)],
        out_specs=[_rows(tm, d)] * 4,
        out_shape=[out] * 4,
        compiler_params=_cparams("parallel", "parallel"),
        name="lru_coef",
    )(xr, xr, xr, cw, gw, gb, lam)


def _scan_body(reverse, has_add, a_ref, b_ref, h0_ref, *rest):
    if has_add:
        add_ref, o_ref, he_ref, h_s, a_s, b_s = rest
    else:
        o_ref, he_ref, h_s, a_s, b_s = rest
    tm, width = a_ref.shape[1], a_ref.shape[2]
    n_groups = tm // SUBLANES

    @pl.when(pl.program_id(1) == 0)
    def _():
        h_s[...] = jnp.broadcast_to(h0_ref[0], h_s.shape)

    a = a_ref[0].reshape(n_groups, SUBLANES, width)
    b = b_ref[0].reshape(n_groups, SUBLANES, width)
    in_group = lax.broadcasted_iota(jnp.int32, a.shape, 1)
    for k in (1, 2, 4):
        if reverse:
            a_sh, b_sh = pltpu.roll(a, SUBLANES - k, 1), pltpu.roll(b, SUBLANES - k, 1)
            valid = in_group < SUBLANES - k
        else:
            a_sh, b_sh = pltpu.roll(a, k, 1), pltpu.roll(b, k, 1)
            valid = in_group >= k
        b = jnp.where(valid, b + a * b_sh, b)
        a = jnp.where(valid, a * a_sh, a)
    a_s[...] = a.reshape(tm, width)
    b_s[...] = b.reshape(tm, width)

    def group(j, carry):
        g = n_groups - 1 - j if reverse else j
        rows = pl.ds(pl.multiple_of(g * SUBLANES, SUBLANES), SUBLANES)
        h = a_s[rows, :] * carry + b_s[rows, :]
        o_ref[0, rows, :] = h + add_ref[0, rows, :] if has_add else h
        last = h[0:1, :] if reverse else h[SUBLANES - 1:SUBLANES, :]
        return jnp.broadcast_to(last, carry.shape)

    carry = lax.fori_loop(0, n_groups, group, h_s[...], unroll=8)
    h_s[...] = carry
    he_ref[0] = carry[0:1, :]


def _scan(a, b, h0, add, reverse, tm):
    nb, t, width = a.shape
    nt = t // tm
    imap = (lambda bb, i: (bb, nt - 1 - i, 0)) if reverse else (lambda bb, i: (bb, i, 0))
    blk = pl.BlockSpec((1, tm, width), imap)
    hspec = pl.BlockSpec((1, 1, width), lambda bb, i: (bb, 0, 0))
    ins = [a, b, h0] + ([add] if add is not None else [])
    specs = [blk, blk, hspec] + ([blk] if add is not None else [])
    return pl.pallas_call(
        functools.partial(_scan_body, reverse, add is not None),
        grid=(nb, nt),
        in_specs=specs,
        out_specs=[blk, hspec],
        out_shape=[jax.ShapeDtypeStruct((nb, t, width), F32), jax.ShapeDtypeStruct((nb, 1, width), F32)],
        scratch_shapes=[pltpu.VMEM((SUBLANES, width), F32), pltpu.VMEM((tm, width), F32),
                        pltpu.VMEM((tm, width), F32)],
        compiler_params=_cparams("parallel", "arbitrary"),
        name="lru_scan_rev" if reverse else "lru_scan_fwd",
    )(*ins)


def _block_tail_body(has_r, x_ref, xp_ref, xn_ref, o_ref, op_ref, on_ref, *rest):
    if has_r:
        r_ref, rp_ref, rn_ref = rest[:3]
        rest = rest[3:]
    (wo_ref, g1_ref, gain_ref, sh_ref, sc_ref, g2_ref, wu_ref, cw_ref, wd_ref, y_ref, x1_s, h_s, act_s) = rest
    i = pl.program_id(1)
    tm = x_ref.shape[1]
    d_ff = wd_ref.shape[0]
    gain, sh, sc, g1 = gain_ref[...], sh_ref[0], sc_ref[0], g1_ref[0]

    def mixed(o, r):
        return o if r is None else (o.astype(F32) * r).astype(BF16)

    wo = wo_ref[...]
    for lo in range(0, tm, tm // 2):
        rows = slice(lo, lo + tm // 2)
        x1 = x_ref[0, rows, :] + g1 * _dot(mixed(o_ref[0, rows, :], r_ref[0, rows, :] if has_r else None), wo)
        x1_s[rows, :] = x1
        h_s[rows, :] = _modnorm(x1, gain, sh, sc).astype(BF16)
    o_halo = jnp.concatenate([op_ref[0], on_ref[0]], axis=0)
    r_halo = jnp.concatenate([rp_ref[0], rn_ref[0]], axis=0) if has_r else None
    x_halo = jnp.concatenate([xp_ref[0], xn_ref[0]], axis=0)
    x1_halo = x_halo + g1 * _dot(mixed(o_halo, r_halo), wo)
    h_s[tm:tm + 2 * FFN_HALO, :] = _modnorm(x1_halo, gain, sh, sc).astype(BF16)
    pm = (i > 0).astype(F32)
    nm = (i < pl.num_programs(1) - 1).astype(F32)

    def up(c):
        lo = c * FF_CHUNK
        return (_dot(h_s[...], wu_ref[:, lo:lo + FF_CHUNK]),
                _dot(h_s[0:tm, :], wu_ref[:, d_ff + lo:d_ff + lo + FF_CHUNK]))

    n_chunks = d_ff // FF_CHUNK
    split = (n_chunks + 1) // 2 * FF_CHUNK
    down_first = None
    nxt = up(0)
    for c in range(n_chunks):
        ge, val = nxt
        if c + 1 < n_chunks:
            nxt = up(c + 1)
        lo = c * FF_CHUNK
        g = ge[0:tm]
        gp = ge[tm + FFN_HALO - 1:tm + FFN_HALO] * pm
        gn = ge[tm + FFN_HALO:tm + FFN_HALO + 1] * nm
        row = lax.broadcasted_iota(jnp.int32, g.shape, 0)
        gd = jnp.where(row == 0, gp, pltpu.roll(g, 1, 0))
        gu = jnp.where(row == tm - 1, gn, pltpu.roll(g, tm - 1, 0))
        cw = cw_ref[:, lo:lo + FF_CHUNK]
        y = cw[0:1] * gd + cw[1:2] * g + cw[2:3] * gu + cw[3:4]
        act_s[:, lo:lo + FF_CHUNK] = (y * jax.nn.sigmoid(y) * val).astype(BF16)
        if lo + FF_CHUNK == split:
            down_first = _dot(act_s[:, :split], wd_ref[:split, :])
    down = down_first + _dot(act_s[:, split:], wd_ref[split:, :])
    y_ref[0] = x1_s[...] + g2_ref[0] * down


def _block_tail(x, o, r, w_o, g1, gain, sh, sc, g2, w, tm):
    b, l, d = x.shape
    dk = o.shape[2]
    wu, cw, wd = w
    xs = [_rows(tm, d), *_halo_specs(tm, l, d, FFN_HALO)]
    os_ = [_rows(tm, dk), *_halo_specs(tm, l, dk, FFN_HALO)]
    ins = [x, x, x, o, o, o] + ([r, r, r] if r is not None else [])
    specs = xs + os_ + (os_ if r is not None else [])
    ins += [w_o, g1, gain, sh, sc, g2, wu, cw, wd]
    specs += [_const(w_o.shape), _vec(d), _const(gain.shape), _vec(d), _vec(d), _vec(d),
              _const(wu.shape), _const(cw.shape), _const(wd.shape)]
    return pl.pallas_call(
        functools.partial(_block_tail_body, r is not None),
        grid=(b, l // tm),
        in_specs=specs,
        out_specs=_rows(tm, d),
        out_shape=jax.ShapeDtypeStruct((b, l, d), F32),
        scratch_shapes=[pltpu.VMEM((tm, d), F32), pltpu.VMEM((tm + 2 * FFN_HALO, d), BF16),
                        pltpu.VMEM((tm, wd.shape[0]), BF16)],
        compiler_params=_cparams("parallel", "parallel"),
        name="block_tail",
    )(*ins)


def _rope_cs(n_tokens, rot_dim):
    pos = jnp.arange(n_tokens)
    row = (pos // GRID_W).astype(F32)
    col = (pos % GRID_W).astype(F32)
    n_freq = rot_dim // 4
    inv = ROPE_BASE ** (-jnp.arange(n_freq, dtype=F32) / n_freq)
    ar, ac = row[:, None] * inv, col[:, None] * inv
    cos = jnp.concatenate([jnp.cos(ar), jnp.cos(ar), jnp.cos(ac), jnp.cos(ac)], axis=1)
    sin = jnp.concatenate([-jnp.sin(ar), jnp.sin(ar), -jnp.sin(ac), jnp.sin(ac)], axis=1)
    return cos, sin


def _pad_rows(a, rows):
    return jnp.pad(a, ((0, rows - a.shape[0]), (0, 0)))


def _pair_swap(n, half):
    i = jnp.arange(n)
    return jnp.where((i & half) == 0, i + half, i - half)


def _perm_matrix(half):
    return (jnp.arange(MXU_COLS)[:, None] == _pair_swap(MXU_COLS, half)[None, :]).astype(BF16)


def _group_ones(size):
    g = jnp.arange(MXU_COLS) // size
    return (g[:, None] == g[None, :]).astype(BF16)


def _gain_tables(gain, cos, sin, half):
    return gain[None, :] * cos, gain[_pair_swap(LANES, half)][None, :] * sin, gain[None, :]


def kernel(x, c, ctx, c_ctx, norm1, norm2, mod_w, mod_b, swa_w_qkv, swa_q_gain, swa_k_gain, swa_sink, swa_w_o, mla_w_down, mla_q_lora_gain, mla_w_uq, mla_kv_lora_gain, mla_w_uk, mla_w_uv, mla_q_gain, mla_k_gain, mla_w_o, lru_w_in, lru_conv_w, lru_conv_b, lru_gate_w, lru_gate_b, lru_lam, lru_w_out, ffn_w_up, ffn_conv_w, ffn_conv_b, ffn_w_down):
    bsz, seq, d = x.shape
    n_ctx = ctx.shape[1]
    depth = norm1.shape[0]
    tm_lat, tm_ctx = ROW_TILE, n_ctx

    cond = _pad_rows(jnp.concatenate([c, c_ctx[None, :]], axis=0), 16)
    mods = _mod_all(cond, mod_w, mod_b)

    cos_s, sin_s = _rope_cs(seq, HEAD_DIM)
    cos_s, sin_s = jnp.tile(cos_s, (1, 2)), jnp.tile(sin_s, (1, 2))
    cos_m, sin_m = _rope_cs(seq, MLA_ROPE)
    one = jnp.ones((seq, MLA_NOPE), F32)
    zero = jnp.zeros((seq, MLA_NOPE), F32)
    cos_m = jnp.concatenate([one, cos_m, one[:, :LANES - MLA_QK]], axis=1)
    sin_m = jnp.concatenate([zero, sin_m, zero[:, :LANES - MLA_QK]], axis=1)
    ones64, ones128 = _group_ones(HEAD_DIM), _group_ones(LANES)

    for layer in range(depth):
        kind, idx = layer % 3, layer // 3
        with_ctx_out = layer < depth - 1
        m = mods[layer].reshape(16, 6, d)
        ml = [m[:bsz, j][:, None, :] for j in range(6)]
        mc = [jnp.broadcast_to(m[bsz, j][None, None, :], (bsz, 1, d)) for j in range(6)]
        n1 = norm1[layer][None, :]
        n2 = norm2[layer][None, :]
        r_lat = r_ctx = None
        if kind == 0:
            wqkv = swa_w_qkv[idx]
            nq = SWA_HEADS * HEAD_DIM
            nkv = SWA_KV_HEADS * HEAD_DIM

            def dup(w):
                w = w.reshape(d, SWA_KV_HEADS, 1, HEAD_DIM)
                return jnp.broadcast_to(w, (d, SWA_KV_HEADS, 2, HEAD_DIM)).reshape(d, 2 * nkv).astype(BF16)

            w = (wqkv[:, :nq].astype(BF16), dup(wqkv[:, nq:nq + nkv]), dup(wqkv[:, nq + nkv:]),
                 ones64, ones128, _perm_matrix(HEAD_DIM // 4))
            cq, sq, gq = _gain_tables(jnp.tile(swa_q_gain[idx], 2) * LOG2E, cos_s, sin_s, HEAD_DIM // 4)
            ck, sk, gk = _gain_tables(jnp.tile(swa_k_gain[idx], 2) * (2 * HEAD_DIM) ** 0.5, cos_s, sin_s,
                                      HEAD_DIM // 4)
            sink = swa_sink[idx] * LOG2E
            qc, kc, vc = _swa_proj(ctx, n1, mc[0], mc[1], w, (gq, gk), False, tm_ctx)
            ql, kl, vl = _swa_proj(x, n1, ml[0], ml[1], w, (cq, ck, sq, sk), True, tm_lat)
            o_lat = _swa_attn(sink, ql, kl, vl, kc, vc, ATTN_Q_TILE, ATTN_ROWS)
            o_ctx = _swa_attn(sink, qc, None, None, kc, vc, n_ctx, n_ctx) if with_ctx_out else None
            w_o = swa_w_o[idx].astype(BF16)
        elif kind == 1:
            wdn = mla_w_down[idx]
            zc = jnp.zeros((d, MLA_NOPE), F32)
            wd_pad = jnp.concatenate([wdn[:, :Q_LORA + KV_LORA], zc, wdn[:, Q_LORA + KV_LORA:],
                                      zc[:, :LANES - MLA_QK]], axis=1).astype(BF16)
            wuq = jnp.pad(mla_w_uq[idx].reshape(Q_LORA, MLA_HEADS, MLA_QK),
                          ((0, 0), (0, 0), (0, LANES - MLA_QK))).reshape(Q_LORA, MLA_HEADS * LANES).astype(BF16)
            wuk = jnp.pad(mla_w_uk[idx].reshape(KV_LORA, MLA_HEADS, MLA_NOPE),
                          ((0, 0), (0, 0), (0, LANES - MLA_NOPE))).reshape(KV_LORA, MLA_HEADS * LANES).astype(BF16)
            root = MLA_QK ** 0.5
            gq = jnp.pad(mla_q_gain[idx] * LOG2E, (0, LANES - MLA_QK))
            gk_nope = jnp.pad(mla_k_gain[idx][:MLA_NOPE] * root, (0, LANES - MLA_NOPE))[None, :]
            gk_rope = jnp.pad(mla_k_gain[idx][MLA_NOPE:] * root, (MLA_NOPE, LANES - MLA_QK))
            cq, sq, gq = _gain_tables(gq, cos_m, sin_m, MLA_ROPE // 4)
            ck, sk, gk = _gain_tables(gk_rope, cos_m, sin_m, MLA_ROPE // 4)
            w = (wd_pad, mla_q_lora_gain[idx][None, :], wuq, mla_kv_lora_gain[idx][None, :], wuk,
                 mla_w_uv[idx].astype(BF16), ones128, _perm_matrix(MLA_ROPE // 4), gk_nope)
            qc, kc, vc = _mla_proj(ctx, n1, mc[0], mc[1], w, (gq, gk), False, tm_ctx)
            ql, kl, vl = _mla_proj(x, n1, ml[0], ml[1], w, (cq, ck, sq, sk), True, tm_lat)
            o_lat = _mla_attn(ql, kl, vl, kc, vc, ATTN_Q_TILE, ATTN_ROWS)
            o_ctx = _mla_attn(qc, None, None, kc, vc, n_ctx, n_ctx) if with_ctx_out else None
            w_o = mla_w_o[idx].astype(BF16)
        else:
            w_in = lru_w_in[idx].astype(BF16)
            cw = _pad_rows(jnp.concatenate([lru_conv_w[idx], lru_conv_b[idx][None, :]], axis=0), 8)
            gw = lru_gate_w[idx].reshape(2 * 2 * LRU_BLOCKS, LRU_BW, LRU_BW).astype(BF16)
            gb = _pad_rows(lru_gate_b[idx].reshape(4, d), 8)
            lam = _pad_rows(lru_lam[idx], 8)
            g_ctx, xr_ctx = _lru_in(ctx, n1, mc[0], mc[1], w_in, tm_ctx)
            g_lat, xr_lat = _lru_in(x, n1, ml[0], ml[1], w_in, tm_lat)
            afc, bfc, arc, brc = _lru_coef(xr_ctx, cw, gw, gb, lam, tm_ctx)
            afl, bfl, arl, brl = _lru_coef(xr_lat, cw, gw, gb, lam, tm_lat)
            h0 = jnp.zeros((bsz, 1, d), F32)
            sfc, hfc = _scan(afc, bfc, h0, None, False, tm_ctx)
            r_ctx, hrc = _scan(arc, brc, h0, sfc, True, tm_ctx)
            sfl, _ = _scan(afl, bfl, hfc, None, False, tm_lat)
            r_lat, _ = _scan(arl, brl, hrc, sfl, True, tm_lat)
            o_lat, o_ctx = g_lat, (g_ctx if with_ctx_out else None)
            w_o = lru_w_out[idx].astype(BF16)

        cw_ffn = _pad_rows(jnp.concatenate([ffn_conv_w[layer], ffn_conv_b[layer][None, :]], axis=0), 8)
        wf = (ffn_w_up[layer].astype(BF16), cw_ffn, ffn_w_down[layer].astype(BF16))

        x = _block_tail(x, o_lat, r_lat, w_o, ml[2], n2, ml[3], ml[4], ml[5], wf, tm_lat)
        if with_ctx_out:
            ctx = _block_tail(ctx, o_ctx, r_ctx, w_o, mc[2], n2, mc[3], mc[4], mc[5], wf, tm_ctx)
    return x
```

```python
import functools

import jax
import jax.numpy as jnp
from jax import lax
from jax.experimental import pallas as pl
from jax.experimental.pallas import tpu as pltpu

F32 = jnp.float32
BF16 = jnp.bfloat16

GRID_W = 64
HEAD_DIM = 64
SWA_HEADS = 16
SWA_KV_HEADS = 4
WINDOW = 128
MLA_HEADS = 16
MLA_NOPE = 64
MLA_ROPE = 32
MLA_QK = MLA_NOPE + MLA_ROPE
Q_LORA = 384
KV_LORA = 256
LRU_BLOCKS = 4
LRU_BW = 256
LRU_C = 8.0
D_FF = 2816
ROPE_BASE = 10000.0
EPS = 1e-6
NEG_INF = -1e30
LOG2E = 1.4426950408889634

LANES = 128
MXU_COLS = 256
FF_CHUNK = MXU_COLS
SUBLANES = 8
HALO = SUBLANES
FFN_HALO = 16
MLA_KEY_BLOCK = 512
ROW_TILE = 512
ATTN_ROWS = MXU_COLS
SWA_Q_TILE = 1024
MLA_Q_TILE = 2048
MOD_COLS = 1536
VMEM_LIMIT = 56 * 1024 * 1024


def _cparams(*sem):
    return pltpu.CompilerParams(dimension_semantics=sem, vmem_limit_bytes=VMEM_LIMIT)


def _const(shape):
    nd = len(shape)
    return pl.BlockSpec(shape, lambda *_: (0,) * nd, pipeline_mode=pl.Buffered(1))


def _rows(tm, d):
    return pl.BlockSpec((1, tm, d), lambda b, i: (b, i, 0))


def _vec(d):
    return pl.BlockSpec((1, 1, d), lambda b, i: (b, 0, 0))


def _dot(a, b):
    return jnp.dot(a, b, preferred_element_type=F32)


def _dot_t(a, b):
    return lax.dot_general(a, b, (((1,), (1,)), ((), ())), preferred_element_type=F32)


def _modnorm(x, gain, shift, scale):
    ms = jnp.mean(x * x, axis=-1, keepdims=True)
    y = x * lax.rsqrt(ms + EPS) * gain
    return y * (1.0 + scale) + shift


def _chunk_dot(x, m):
    w = m.shape[0]
    return jnp.concatenate([_dot(x[:, j * w:(j + 1) * w], m) for j in range(x.shape[1] // w)], axis=1)


def _lane_tile(t, width):
    return jnp.concatenate([t] * (width // t.shape[1]), axis=1)


def _norm_rope(x, group_ones, perm, cos, sin, eps_sum):
    ssq = _chunk_dot((x * x).astype(BF16), group_ones)
    y = x * _lane_tile(cos, x.shape[1])
    if sin is not None:
        y = y + _chunk_dot(x.astype(BF16), perm) * _lane_tile(sin, x.shape[1])
    return y * lax.rsqrt(ssq + eps_sum)


def _mod_body(c_ref, w_ref, b_ref, o_ref):
    c = c_ref[...]
    cs = (c * jax.nn.sigmoid(c)).astype(BF16)
    o_ref[0] = _dot(cs, w_ref[0].astype(BF16)) + b_ref[0]


def _mod_all(cond, mod_w, mod_b):
    depth, d, n = mod_w.shape
    tn = MOD_COLS
    rows = cond.shape[0]
    return pl.pallas_call(
        _mod_body,
        grid=(depth, n // tn),
        in_specs=[pl.BlockSpec((rows, d), lambda l, j: (0, 0)),
                  pl.BlockSpec((1, d, tn), lambda l, j: (l, 0, j)),
                  pl.BlockSpec((1, 1, tn), lambda l, j: (l, 0, j))],
        out_specs=pl.BlockSpec((1, rows, tn), lambda l, j: (l, 0, j)),
        out_shape=jax.ShapeDtypeStruct((depth, rows, n), F32),
        compiler_params=_cparams("arbitrary", "arbitrary"),
        name="mod",
    )(cond, mod_w, mod_b.reshape(depth, 1, n))


def _swa_proj_body(rope, x_ref, gain_ref, sh_ref, sc_ref, wq_ref, wk_ref, wv_ref, g64_ref, g128_ref,
                   perm_ref, cq_ref, ck_ref, *rest):
    if rope:
        sq_ref, sk_ref, q_out, k_out, v_out = rest
        sq, sk = sq_ref[...], sk_ref[...]
    else:
        q_out, k_out, v_out = rest
        sq = sk = None
    h = _modnorm(x_ref[0], gain_ref[...], sh_ref[0], sc_ref[0]).astype(BF16)
    q = _dot(h, wq_ref[...])
    k = _dot(h, wk_ref[...])
    v_out[0] = _dot(h, wv_ref[...]).astype(BF16)
    perm = perm_ref[...]
    q_out[0] = _norm_rope(q, g64_ref[...], perm, cq_ref[...], sq, HEAD_DIM * EPS).astype(BF16)
    k_out[0] = _norm_rope(k, g128_ref[...], perm, ck_ref[...], sk, 2 * HEAD_DIM * EPS).astype(BF16)


def _table_specs(tables, rope, tm):
    if rope:
        return [pl.BlockSpec((tm, LANES), lambda bb, i: (i, 0))] * len(tables)
    return [_const(t.shape) for t in tables]


def _swa_proj(x, gain, sh, sc, w, tables, rope, tm):
    b, l, d = x.shape
    wq, wk, wv = w[:3]
    ins = [x, gain, sh, sc] + list(w) + list(tables)
    specs = ([_rows(tm, d), _const(gain.shape), _vec(d), _vec(d)] + [_const(a.shape) for a in w]
             + _table_specs(tables, rope, tm))
    nq, nk = wq.shape[1], wk.shape[1]
    return pl.pallas_call(
        functools.partial(_swa_proj_body, rope),
        grid=(b, l // tm),
        in_specs=specs,
        out_specs=[_rows(tm, nq), _rows(tm, nk), _rows(tm, nk)],
        out_shape=[jax.ShapeDtypeStruct((b, l, nq), BF16), jax.ShapeDtypeStruct((b, l, nk), BF16),
                   jax.ShapeDtypeStruct((b, l, nk), BF16)],
        compiler_params=_cparams("parallel", "parallel"),
        name="swa_proj_lat" if rope else "swa_proj_ctx",
    )(*ins)


def _fold(x, op):
    acc = x[:, :LANES]
    for t in range(1, x.shape[1] // LANES):
        acc = op(acc, x[:, t * LANES:(t + 1) * LANES])
    return acc


def _attend(chains, s_scr):
    def phase1(ci):
        q, parts, _ = chains[ci]
        st = {"mx": None}

        def make(p, off):
            def step():
                k_thunk, _, bias, nk = parts[p]
                s = _dot_t(q, k_thunk())
                if bias is not None:
                    s = s + bias
                s_scr[ci % 2, :, off:off + nk] = s
                f = _fold(s, jnp.maximum)
                st["mx"] = f if st["mx"] is None else jnp.maximum(st["mx"], f)
            return step

        steps, off = [], 0
        for p, part in enumerate(parts):
            steps.append(make(p, off))
            off += part[3]
        return st, steps

    def phase2(ci, st):
        _, parts, sink = chains[ci]
        out = {"acc": [None, None]}

        def make(p, off):
            def step():
                if p == 0:
                    m = jnp.max(st["mx"], axis=-1, keepdims=True)
                    out["m"] = m if sink is None else jnp.maximum(m, sink)
                nk = parts[p][3]
                pr = jnp.exp2(s_scr[ci % 2, :, off:off + nk] - out["m"])
                v = parts[p][1]()
                pv = _dot(pr.astype(BF16), jnp.concatenate([v, jnp.ones_like(v)], axis=1))
                out["acc"][p % 2] = pv if out["acc"][p % 2] is None else out["acc"][p % 2] + pv
            return step

        def finish():
            acc = out["acc"][0] if out["acc"][1] is None else out["acc"][0] + out["acc"][1]
            den = acc[:, LANES:]
            if sink is not None:
                den = den + jnp.exp2(sink - out["m"])
            return acc[:, :LANES] / den

        steps, off = [], 0
        for p, part in enumerate(parts):
            steps.append(make(p, off))
            off += part[3]
        return steps, finish

    results = []
    pending = None
    for ci in range(len(chains) + 1):
        st, steps1 = phase1(ci) if ci < len(chains) else (None, [])
        steps2 = pending[0] if pending else []
        for k in range(max(len(steps1), len(steps2))):
            if k < len(steps1):
                steps1[k]()
            if k < len(steps2):
                steps2[k]()
        if pending:
            results.append(pending[1]())
        pending = phase2(ci, st) if ci < len(chains) else None
    return results


def _swa_attn_body(has_win, tq, sub, seq, sink_ref, q_ref, *refs):
    if has_win:
        kd_ref, vd_ref, kc_ref, vc_ref, o_ref, s_scr = refs
    else:
        kc_ref, vc_ref, o_ref, s_scr = refs
    hk = pl.program_id(1)
    n_ctx = kc_ref.shape[1]
    span = sub + 2 * WINDOW
    low = lax.broadcasted_iota(jnp.int32, (sub, LANES), 1) < HEAD_DIM
    chains = []
    for r in range(tq // sub):
        if has_win:
            s0 = pl.program_id(2) * tq + r * sub
            ws = pl.multiple_of(jnp.clip(s0 - WINDOW, 0, seq - span), LANES)
            qpos = s0 + lax.broadcasted_iota(jnp.int32, (sub, span), 0)
            kpos = ws + lax.broadcasted_iota(jnp.int32, (sub, span), 1)
            bias = jnp.where(jnp.abs(qpos - kpos) <= WINDOW, 0.0, NEG_INF).astype(F32)
        for g in range(4):
            qp = q_ref[0, r * sub:(r + 1) * sub, (g // 2) * LANES:(g // 2 + 1) * LANES]
            qm = jnp.where(low if g % 2 == 0 else jnp.logical_not(low), qp, jnp.zeros_like(qp))
            parts = []
            if has_win:
                parts.append((lambda ws=ws: kd_ref[0, pl.ds(ws, span), :],
                              lambda ws=ws: vd_ref[0, pl.ds(ws, span), :], bias, span))
            parts.append((lambda: kc_ref[0], lambda: vc_ref[0], None, n_ctx))
            chains.append((qm, parts, sink_ref[hk * 4 + g]))
    o = _attend(chains, s_scr)
    for r in range(tq // sub):
        o_ref[0, r * sub:(r + 1) * sub, :] = jnp.concatenate(
            [jnp.where(low, o[4 * r], o[4 * r + 1]), jnp.where(low, o[4 * r + 2], o[4 * r + 3])],
            axis=1).astype(o_ref.dtype)


def _swa_attn(sink, q, kd, vd, kcd, vcd, tq, sub):
    b, l, dq = q.shape
    n_ctx = kcd.shape[1]
    has_win = kd is not None
    gw = dq // SWA_KV_HEADS
    ins = [sink, q]
    specs = [pl.BlockSpec(memory_space=pltpu.SMEM),
             pl.BlockSpec((1, tq, gw), lambda bb, h, i: (bb, i, h))]
    if has_win:
        ins += [kd, vd]
        specs += [pl.BlockSpec((1, l, LANES), lambda bb, h, i: (bb, 0, h))] * 2
    ins += [kcd, vcd]
    specs += [pl.BlockSpec((1, n_ctx, LANES), lambda bb, h, i: (bb, 0, h))] * 2
    return pl.pallas_call(
        functools.partial(_swa_attn_body, has_win, tq, sub, l),
        grid=(b, SWA_KV_HEADS, l // tq),
        in_specs=specs,
        out_specs=pl.BlockSpec((1, tq, gw), lambda bb, h, i: (bb, i, h)),
        out_shape=jax.ShapeDtypeStruct((b, l, dq), BF16),
        scratch_shapes=[pltpu.VMEM((2, sub, n_ctx + (sub + 2 * WINDOW if has_win else 0)), F32)],
        compiler_params=_cparams("parallel", "parallel", "parallel"),
        name="swa_attn_lat" if has_win else "swa_attn_ctx",
    )(*ins)


def _mla_proj_body(rope, x_ref, gain_ref, sh_ref, sc_ref, wd_ref, gql_ref, wuq_ref, gkvl_ref,
                   wuk_ref, wuv_ref, g128_ref, perm_ref, gkn_ref, cq_ref, ck_ref, *rest):
    if rope:
        sq_ref, sk_ref, q_out, k_out, v_out = rest
        sq, sk = sq_ref[...], sk_ref[...]
    else:
        q_out, k_out, v_out = rest
        sq = sk = None
    h = _modnorm(x_ref[0], gain_ref[...], sh_ref[0], sc_ref[0]).astype(BF16)
    t = _dot(h, wd_ref[...])
    cq = t[:, :Q_LORA]
    ckv = t[:, Q_LORA:Q_LORA + KV_LORA]
    kr = t[:, Q_LORA + KV_LORA:]
    cq = (cq * lax.rsqrt(jnp.mean(cq * cq, axis=-1, keepdims=True) + EPS) * gql_ref[...]).astype(BF16)
    ckv = (ckv * lax.rsqrt(jnp.mean(ckv * ckv, axis=-1, keepdims=True) + EPS) * gkvl_ref[...]).astype(BF16)
    qa = _dot(cq, wuq_ref[...])
    ka = _dot(ckv, wuk_ref[...])
    v_out[0] = _dot(ckv, wuv_ref[...]).astype(BF16)
    g128, perm = g128_ref[...], perm_ref[...]
    q = _norm_rope(qa, g128, perm, cq_ref[...], sq, MLA_QK * EPS).astype(BF16)
    kr_rot = kr * ck_ref[...]
    if rope:
        kr_rot = kr_rot + _dot(kr.astype(BF16), perm[:LANES, :LANES]) * sk
    width = ka.shape[1]
    kraw = ka + _lane_tile(kr, width)
    ssq = _chunk_dot((kraw * kraw).astype(BF16), g128)
    k = ((ka * _lane_tile(gkn_ref[...], width) + _lane_tile(kr_rot, width))
         * lax.rsqrt(ssq + MLA_QK * EPS)).astype(BF16)
    for hh in range(MLA_HEADS):
        q_out[0, hh] = q[:, hh * LANES:(hh + 1) * LANES]
        k_out[0, hh] = k[:, hh * LANES:(hh + 1) * LANES]


def _mla_proj(x, gain, sh, sc, w, tables, rope, tm):
    b, l, d = x.shape
    ins = [x, gain, sh, sc] + list(w) + list(tables)
    specs = ([_rows(tm, d), _const(gain.shape), _vec(d), _vec(d)] + [_const(a.shape) for a in w]
             + _table_specs(tables, rope, tm))
    hspec = pl.BlockSpec((1, MLA_HEADS, tm, LANES), lambda bb, i: (bb, 0, i, 0))
    hshape = jax.ShapeDtypeStruct((b, MLA_HEADS, l, LANES), BF16)
    nv = w[5].shape[1]
    return pl.pallas_call(
        functools.partial(_mla_proj_body, rope),
        grid=(b, l // tm),
        in_specs=specs,
        out_specs=[hspec, hspec, _rows(tm, nv)],
        out_shape=[hshape, hshape, jax.ShapeDtypeStruct((b, l, nv), BF16)],
        compiler_params=_cparams("parallel", "parallel"),
        name="mla_proj_lat" if rope else "mla_proj_ctx",
    )(*ins)


def _mla_attn_body(has_lat, tk, sub, q_ref, *refs):
    if has_lat:
        kl_ref, vl_ref, kc_ref, vc_ref, o_ref, s_scr = refs
    else:
        kc_ref, vc_ref, o_ref, s_scr = refs
    tq = q_ref.shape[2]
    n_ctx = kc_ref.shape[2]
    low = lax.broadcasted_iota(jnp.int32, (sub, LANES), 1) < MLA_NOPE
    chains = []
    for r in range(tq // sub):
        for e in range(2):
            parts = []
            if has_lat:
                for j in range(kl_ref.shape[2] // tk):
                    parts.append((lambda e=e, j=j: kl_ref[0, e, j * tk:(j + 1) * tk, :],
                                  lambda j=j: vl_ref[0, j * tk:(j + 1) * tk, :], None, tk))
            parts.append((lambda e=e: kc_ref[0, e], lambda: vc_ref[0], None, n_ctx))
            chains.append((q_ref[0, e, r * sub:(r + 1) * sub, :], parts, None))
    o = _attend(chains, s_scr)
    for r in range(tq // sub):
        o_ref[0, r * sub:(r + 1) * sub, :] = jnp.where(low, o[2 * r], o[2 * r + 1]).astype(o_ref.dtype)


def _mla_attn(q, kl, vl, kc, vc, tq, sub):
    b, nh, l, _ = q.shape
    n_ctx = kc.shape[2]
    has_lat = kl is not None
    ins = [q]
    specs = [pl.BlockSpec((1, 2, tq, LANES), lambda bb, j, i: (bb, j, i, 0))]
    if has_lat:
        ins += [kl, vl]
        specs += [pl.BlockSpec((1, 2, l, LANES), lambda bb, j, i: (bb, j, 0, 0)),
                  pl.BlockSpec((1, l, LANES), lambda bb, j, i: (bb, 0, j))]
    ins += [kc, vc]
    specs += [pl.BlockSpec((1, 2, n_ctx, LANES), lambda bb, j, i: (bb, j, 0, 0)),
              pl.BlockSpec((1, n_ctx, LANES), lambda bb, j, i: (bb, 0, j))]
    return pl.pallas_call(
        functools.partial(_mla_attn_body, has_lat, MLA_KEY_BLOCK, sub),
        grid=(b, nh // 2, l // tq),
        in_specs=specs,
        out_specs=pl.BlockSpec((1, tq, LANES), lambda bb, j, i: (bb, i, j)),
        out_shape=jax.ShapeDtypeStruct((b, l, nh * MLA_NOPE), BF16),
        scratch_shapes=[pltpu.VMEM((2, sub, n_ctx + (l if has_lat else 0)), F32)],
        compiler_params=_cparams("parallel", "parallel", "parallel"),
        name="mla_attn_lat" if has_lat else "mla_attn_ctx",
    )(*ins)


def _lru_in_body(x_ref, gain_ref, sh_ref, sc_ref, w_ref, g_out, x_out):
    h = _modnorm(x_ref[0], gain_ref[...], sh_ref[0], sc_ref[0]).astype(BF16)
    u = _dot(h, w_ref[...])
    width = u.shape[1] // 2
    g_out[0] = jax.nn.gelu(u[:, :width], approximate=True).astype(BF16)
    x_out[0] = u[:, width:]


def _lru_in(x, gain, sh, sc, w_in, tm):
    b, l, d = x.shape
    width = w_in.shape[1] // 2
    return pl.pallas_call(
        _lru_in_body,
        grid=(b, l // tm),
        in_specs=[_rows(tm, d), _const(gain.shape), _vec(d), _vec(d), _const(w_in.shape)],
        out_specs=[_rows(tm, width), _rows(tm, width)],
        out_shape=[jax.ShapeDtypeStruct((b, l, width), BF16), jax.ShapeDtypeStruct((b, l, width), F32)],
        compiler_params=_cparams("parallel", "parallel"),
        name="lru_in",
    )(x, gain, sh, sc, w_in)


def _halo_specs(tm, l, d, halo=HALO):
    nb = tm // halo
    last = l // halo - 1
    prev = pl.BlockSpec((1, halo, d), lambda b, i: (b, jnp.maximum(i * nb - 1, 0), 0))
    nxt = pl.BlockSpec((1, halo, d), lambda b, i: (b, jnp.minimum((i + 1) * nb, last), 0))
    return prev, nxt


def _lru_coef_body(x_ref, xp_ref, xn_ref, cw_ref, gw_ref, gb_ref, lam_ref, af_out, bf_out, ar_out, br_out):
    i = pl.program_id(1)
    tm = x_ref.shape[1]
    x = x_ref[0]
    pm = (i > 0).astype(F32)
    nm = (i < pl.num_programs(1) - 1).astype(F32)
    p2 = xp_ref[0, HALO - 2:HALO - 1, :] * pm
    p1 = xp_ref[0, HALO - 1:HALO, :] * pm
    n1 = xn_ref[0, 0:1, :] * nm
    row = lax.broadcasted_iota(jnp.int32, x.shape, 0)
    xm1 = jnp.where(row == 0, p1, pltpu.roll(x, 1, 0))
    xm2 = jnp.where(row == 0, p2, jnp.where(row == 1, p1, pltpu.roll(x, 2, 0)))
    xp1 = jnp.where(row == tm - 1, n1, pltpu.roll(x, tm - 1, 0))
    cw = cw_ref[...]
    xc = cw[0:1] * xm2 + cw[1:2] * xm1 + cw[2:3] * x + cw[3:4] * xp1 + cw[4:5]
    xcb = xc.astype(BF16)
    outs = ((af_out, bf_out), (ar_out, br_out))
    for d in range(2):
        gates = []
        for k in range(2):
            g = jnp.concatenate(
                [_dot(xcb[:, n * LRU_BW:(n + 1) * LRU_BW], gw_ref[(d * 2 + k) * LRU_BLOCKS + n])
                 for n in range(LRU_BLOCKS)], axis=1)
            gates.append(jax.nn.sigmoid(g + gb_ref[d * 2 + k:d * 2 + k + 1, :]))
        r, ig = gates
        nl = -lam_ref[d:d + 1, :]
        softplus = jnp.maximum(nl, 0.0) + jnp.log1p(jnp.exp(-jnp.abs(nl)))
        log_a = (-LRU_C) * r * softplus
        a = jnp.exp(log_a)
        outs[d][0][0] = a
        outs[d][1][0] = jnp.sqrt(-jnp.tanh(log_a) * (a * a + 1.0)) * (ig * xc)


def _lru_coef(xr, cw, gw, gb, lam, tm):
    b, l, d = xr.shape
    prev, nxt = _halo_specs(tm, l, d)
    out = jax.ShapeDtypeStruct((b, l, d), F32)
    return pl.pallas_call(
        _lru_coef_body,
        grid=(b, l // tm),
        in_specs=[_rows(tm, d), prev, nxt, _const(cw.shape), _const(gw.shape), _const(gb.shape),
                  _const(lam.shape)],
        out_specs=[_rows(tm, d)] * 4,
        out_shape=[out] * 4,
        compiler_params=_cparams("parallel", "parallel"),
        name="lru_coef",
    )(xr, xr, xr, cw, gw, gb, lam)


def _scan_body(reverse, has_add, a_ref, b_ref, h0_ref, *rest):
    if has_add:
        add_ref, o_ref, he_ref, h_s, a_s, b_s = rest
    else:
        o_ref, he_ref, h_s, a_s, b_s = rest
    tm, width = a_ref.shape[1], a_ref.shape[2]
    n_groups = tm // SUBLANES

    @pl.when(pl.program_id(1) == 0)
    def _():
        h_s[...] = jnp.broadcast_to(h0_ref[0], h_s.shape)

    a = a_ref[0].reshape(n_groups, SUBLANES, width)
    b = b_ref[0].reshape(n_groups, SUBLANES, width)
    in_group = lax.broadcasted_iota(jnp.int32, a.shape, 1)
    for k in (1, 2, 4):
        if reverse:
            a_sh, b_sh = pltpu.roll(a, SUBLANES - k, 1), pltpu.roll(b, SUBLANES - k, 1)
            valid = in_group < SUBLANES - k
        else:
            a_sh, b_sh = pltpu.roll(a, k, 1), pltpu.roll(b, k, 1)
            valid = in_group >= k
        b = jnp.where(valid, b + a * b_sh, b)
        a = jnp.where(valid, a * a_sh, a)
    a_s[...] = a.reshape(tm, width)
    b_s[...] = b.reshape(tm, width)

    def group(j, carry):
        g = n_groups - 1 - j if reverse else j
        rows = pl.ds(pl.multiple_of(g * SUBLANES, SUBLANES), SUBLANES)
        h = a_s[rows, :] * carry + b_s[rows, :]
        o_ref[0, rows, :] = h + add_ref[0, rows, :] if has_add else h
        last = h[0:1, :] if reverse else h[SUBLANES - 1:SUBLANES, :]
        return jnp.broadcast_to(last, carry.shape)

    carry = lax.fori_loop(0, n_groups, group, h_s[...], unroll=8)
    h_s[...] = carry
    he_ref[0] = carry[0:1, :]


def _scan(a, b, h0, add, reverse, tm):
    nb, t, width = a.shape
    nt = t // tm
    imap = (lambda bb, i: (bb, nt - 1 - i, 0)) if reverse else (lambda bb, i: (bb, i, 0))
    blk = pl.BlockSpec((1, tm, width), imap)
    hspec = pl.BlockSpec((1, 1, width), lambda bb, i: (bb, 0, 0))
    ins = [a, b, h0] + ([add] if add is not None else [])
    specs = [blk, blk, hspec] + ([blk] if add is not None else [])
    return pl.pallas_call(
        functools.partial(_scan_body, reverse, add is not None),
        grid=(nb, nt),
        in_specs=specs,
        out_specs=[blk, hspec],
        out_shape=[jax.ShapeDtypeStruct((nb, t, width), F32), jax.ShapeDtypeStruct((nb, 1, width), F32)],
        scratch_shapes=[pltpu.VMEM((SUBLANES, width), F32), pltpu.VMEM((tm, width), F32),
                        pltpu.VMEM((tm, width), F32)],
        compiler_params=_cparams("parallel", "arbitrary"),
        name="lru_scan_rev" if reverse else "lru_scan_fwd",
    )(*ins)


def _block_tail_body(has_r, x_ref, xp_ref, xn_ref, o_ref, op_ref, on_ref, *rest):
    if has_r:
        r_ref, rp_ref, rn_ref = rest[:3]
        rest = rest[3:]
    (wo_ref, g1_ref, gain_ref, sh_ref, sc_ref, g2_ref, wu_ref, cw_ref, wd_ref, y_ref, x1_s, h_s, act_s) = rest
    i = pl.program_id(1)
    tm = x_ref.shape[1]
    d_ff = wd_ref.shape[0]
    gain, sh, sc, g1 = gain_ref[...], sh_ref[0], sc_ref[0], g1_ref[0]

    def mixed(o, r):
        return o if r is None else (o.astype(F32) * r).astype(BF16)

    wo = wo_ref[...]
    for lo in range(0, tm, tm // 2):
        rows = slice(lo, lo + tm // 2)
        x1 = x_ref[0, rows, :] + g1 * _dot(mixed(o_ref[0, rows, :], r_ref[0, rows, :] if has_r else None), wo)
        x1_s[rows, :] = x1
        h_s[rows, :] = _modnorm(x1, gain, sh, sc).astype(BF16)
    o_halo = jnp.concatenate([op_ref[0], on_ref[0]], axis=0)
    r_halo = jnp.concatenate([rp_ref[0], rn_ref[0]], axis=0) if has_r else None
    x_halo = jnp.concatenate([xp_ref[0], xn_ref[0]], axis=0)
    x1_halo = x_halo + g1 * _dot(mixed(o_halo, r_halo), wo)
    h_s[tm:tm + 2 * FFN_HALO, :] = _modnorm(x1_halo, gain, sh, sc).astype(BF16)
    pm = (i > 0).astype(F32)
    nm = (i < pl.num_programs(1) - 1).astype(F32)

    def up(c):
        lo = c * FF_CHUNK
        return (_dot(h_s[...], wu_ref[:, lo:lo + FF_CHUNK]),
                _dot(h_s[0:tm, :], wu_ref[:, d_ff + lo:d_ff + lo + FF_CHUNK]))

    n_chunks = d_ff // FF_CHUNK
    split = (n_chunks + 1) // 2 * FF_CHUNK
    down_first = None
    nxt = up(0)
    for c in range(n_chunks):
        ge, val = nxt
        if c + 1 < n_chunks:
            nxt = up(c + 1)
        lo = c * FF_CHUNK
        g = ge[0:tm]
        gp = ge[tm + FFN_HALO - 1:tm + FFN_HALO] * pm
        gn = ge[tm + FFN_HALO:tm + FFN_HALO + 1] * nm
        row = lax.broadcasted_iota(jnp.int32, g.shape, 0)
        gd = jnp.where(row == 0, gp, pltpu.roll(g, 1, 0))
        gu = jnp.where(row == tm - 1, gn, pltpu.roll(g, tm - 1, 0))
        cw = cw_ref[:, lo:lo + FF_CHUNK]
        y = cw[0:1] * gd + cw[1:2] * g + cw[2:3] * gu + cw[3:4]
        act_s[:, lo:lo + FF_CHUNK] = (y * jax.nn.sigmoid(y) * val).astype(BF16)
        if lo + FF_CHUNK == split:
            down_first = _dot(act_s[:, :split], wd_ref[:split, :])
    down = down_first + _dot(act_s[:, split:], wd_ref[split:, :])
    y_ref[0] = x1_s[...] + g2_ref[0] * down


def _block_tail(x, o, r, w_o, g1, gain, sh, sc, g2, w, tm):
    b, l, d = x.shape
    dk = o.shape[2]
    wu, cw, wd = w
    xs = [_rows(tm, d), *_halo_specs(tm, l, d, FFN_HALO)]
    os_ = [_rows(tm, dk), *_halo_specs(tm, l, dk, FFN_HALO)]
    ins = [x, x, x, o, o, o] + ([r, r, r] if r is not None else [])
    specs = xs + os_ + (os_ if r is not None else [])
    ins += [w_o, g1, gain, sh, sc, g2, wu, cw, wd]
    specs += [_const(w_o.shape), _vec(d), _const(gain.shape), _vec(d), _vec(d), _vec(d),
              _const(wu.shape), _const(cw.shape), _const(wd.shape)]
    return pl.pallas_call(
        functools.partial(_block_tail_body, r is not None),
        grid=(b, l // tm),
        in_specs=specs,
        out_specs=_rows(tm, d),
        out_shape=jax.ShapeDtypeStruct((b, l, d), F32),
        scratch_shapes=[pltpu.VMEM((tm, d), F32), pltpu.VMEM((tm + 2 * FFN_HALO, d), BF16),
                        pltpu.VMEM((tm, wd.shape[0]), BF16)],
        compiler_params=_cparams("parallel", "parallel"),
        name="block_tail",
    )(*ins)


def _rope_cs(n_tokens, rot_dim):
    pos = jnp.arange(n_tokens)
    row = (pos // GRID_W).astype(F32)
    col = (pos % GRID_W).astype(F32)
    n_freq = rot_dim // 4
    inv = ROPE_BASE ** (-jnp.arange(n_freq, dtype=F32) / n_freq)
    ar, ac = row[:, None] * inv, col[:, None] * inv
    cos = jnp.concatenate([jnp.cos(ar), jnp.cos(ar), jnp.cos(ac), jnp.cos(ac)], axis=1)
    sin = jnp.concatenate([-jnp.sin(ar), jnp.sin(ar), -jnp.sin(ac), jnp.sin(ac)], axis=1)
    return cos, sin


def _pad_rows(a, rows):
    return jnp.pad(a, ((0, rows - a.shape[0]), (0, 0)))


def _pair_swap(n, half):
    i = jnp.arange(n)
    return jnp.where((i & half) == 0, i + half, i - half)


def _perm_matrix(half):
    return (jnp.arange(MXU_COLS)[:, None] == _pair_swap(MXU_COLS, half)[None, :]).astype(BF16)


def _group_ones(size):
    g = jnp.arange(MXU_COLS) // size
    return (g[:, None] == g[None, :]).astype(BF16)


def _gain_tables(gain, cos, sin, half):
    return gain[None, :] * cos, gain[_pair_swap(LANES, half)][None, :] * sin, gain[None, :]


def kernel(x, c, ctx, c_ctx, norm1, norm2, mod_w, mod_b, swa_w_qkv, swa_q_gain, swa_k_gain, swa_sink, swa_w_o, mla_w_down, mla_q_lora_gain, mla_w_uq, mla_kv_lora_gain, mla_w_uk, mla_w_uv, mla_q_gain, mla_k_gain, mla_w_o, lru_w_in, lru_conv_w, lru_conv_b, lru_gate_w, lru_gate_b, lru_lam, lru_w_out, ffn_w_up, ffn_conv_w, ffn_conv_b, ffn_w_down):
    bsz, seq, d = x.shape
    n_ctx = ctx.shape[1]
    depth = norm1.shape[0]
    tm_lat, tm_ctx = ROW_TILE, n_ctx

    cond = _pad_rows(jnp.concatenate([c, c_ctx[None, :]], axis=0), 16)
    mods = _mod_all(cond, mod_w, mod_b)

    cos_s, sin_s = _rope_cs(seq, HEAD_DIM)
    cos_s, sin_s = jnp.tile(cos_s, (1, 2)), jnp.tile(sin_s, (1, 2))
    cos_m, sin_m = _rope_cs(seq, MLA_ROPE)
    one = jnp.ones((seq, MLA_NOPE), F32)
    zero = jnp.zeros((seq, MLA_NOPE), F32)
    cos_m = jnp.concatenate([one, cos_m, one[:, :LANES - MLA_QK]], axis=1)
    sin_m = jnp.concatenate([zero, sin_m, zero[:, :LANES - MLA_QK]], axis=1)
    ones64, ones128 = _group_ones(HEAD_DIM), _group_ones(LANES)

    for layer in range(depth):
        kind, idx = layer % 3, layer // 3
        with_ctx_out = layer < depth - 1
        m = mods[layer].reshape(16, 6, d)
        ml = [m[:bsz, j][:, None, :] for j in range(6)]
        mc = [jnp.broadcast_to(m[bsz, j][None, None, :], (bsz, 1, d)) for j in range(6)]
        n1 = norm1[layer][None, :]
        n2 = norm2[layer][None, :]
        r_lat = r_ctx = None
        if kind == 0:
            wqkv = swa_w_qkv[idx]
            nq = SWA_HEADS * HEAD_DIM
            nkv = SWA_KV_HEADS * HEAD_DIM

            def dup(w):
                w = w.reshape(d, SWA_KV_HEADS, 1, HEAD_DIM)
                return jnp.broadcast_to(w, (d, SWA_KV_HEADS, 2, HEAD_DIM)).reshape(d, 2 * nkv).astype(BF16)

            w = (wqkv[:, :nq].astype(BF16), dup(wqkv[:, nq:nq + nkv]), dup(wqkv[:, nq + nkv:]),
                 ones64, ones128, _perm_matrix(HEAD_DIM // 4))
            cq, sq, gq = _gain_tables(jnp.tile(swa_q_gain[idx], 2) * LOG2E, cos_s, sin_s, HEAD_DIM // 4)
            ck, sk, gk = _gain_tables(jnp.tile(swa_k_gain[idx], 2) * (2 * HEAD_DIM) ** 0.5, cos_s, sin_s,
                                      HEAD_DIM // 4)
            sink = swa_sink[idx] * LOG2E
            qc, kc, vc = _swa_proj(ctx, n1, mc[0], mc[1], w, (gq, gk), False, tm_ctx)
            ql, kl, vl = _swa_proj(x, n1, ml[0], ml[1], w, (cq, ck, sq, sk), True, tm_lat)
            o_lat = _swa_attn(sink, ql, kl, vl, kc, vc, SWA_Q_TILE, ATTN_ROWS)
            o_ctx = _swa_attn(sink, qc, None, None, kc, vc, n_ctx, n_ctx) if with_ctx_out else None
            w_o = swa_w_o[idx].astype(BF16)
        elif kind == 1:
            wdn = mla_w_down[idx]
            zc = jnp.zeros((d, MLA_NOPE), F32)
            wd_pad = jnp.concatenate([wdn[:, :Q_LORA + KV_LORA], zc, wdn[:, Q_LORA + KV_LORA:],
                                      zc[:, :LANES - MLA_QK]], axis=1).astype(BF16)
            wuq = jnp.pad(mla_w_uq[idx].reshape(Q_LORA, MLA_HEADS, MLA_QK),
                          ((0, 0), (0, 0), (0, LANES - MLA_QK))).reshape(Q_LORA, MLA_HEADS * LANES).astype(BF16)
            wuk = jnp.pad(mla_w_uk[idx].reshape(KV_LORA, MLA_HEADS, MLA_NOPE),
                          ((0, 0), (0, 0), (0, LANES - MLA_NOPE))).reshape(KV_LORA, MLA_HEADS * LANES).astype(BF16)
            root = MLA_QK ** 0.5
            gq = jnp.pad(mla_q_gain[idx] * LOG2E, (0, LANES - MLA_QK))
            gk_nope = jnp.pad(mla_k_gain[idx][:MLA_NOPE] * root, (0, LANES - MLA_NOPE))[None, :]
            gk_rope = jnp.pad(mla_k_gain[idx][MLA_NOPE:] * root, (MLA_NOPE, LANES - MLA_QK))
            cq, sq, gq = _gain_tables(gq, cos_m, sin_m, MLA_ROPE // 4)
            ck, sk, gk = _gain_tables(gk_rope, cos_m, sin_m, MLA_ROPE // 4)
            w = (wd_pad, mla_q_lora_gain[idx][None, :], wuq, mla_kv_lora_gain[idx][None, :], wuk,
                 mla_w_uv[idx].astype(BF16), ones128, _perm_matrix(MLA_ROPE // 4), gk_nope)
            qc, kc, vc = _mla_proj(ctx, n1, mc[0], mc[1], w, (gq, gk), False, tm_ctx)
            ql, kl, vl = _mla_proj(x, n1, ml[0], ml[1], w, (cq, ck, sq, sk), True, tm_lat)
            o_lat = _mla_attn(ql, kl, vl, kc, vc, MLA_Q_TILE, ATTN_ROWS)
            o_ctx = _mla_attn(qc, None, None, kc, vc, n_ctx, n_ctx) if with_ctx_out else None
            w_o = mla_w_o[idx].astype(BF16)
        else:
            w_in = lru_w_in[idx].astype(BF16)
            cw = _pad_rows(jnp.concatenate([lru_conv_w[idx], lru_conv_b[idx][None, :]], axis=0), 8)
            gw = lru_gate_w[idx].reshape(2 * 2 * LRU_BLOCKS, LRU_BW, LRU_BW).astype(BF16)
            gb = _pad_rows(lru_gate_b[idx].reshape(4, d), 8)
            lam = _pad_rows(lru_lam[idx], 8)
            g_ctx, xr_ctx = _lru_in(ctx, n1, mc[0], mc[1], w_in, tm_ctx)
            g_lat, xr_lat = _lru_in(x, n1, ml[0], ml[1], w_in, tm_lat)
            afc, bfc, arc, brc = _lru_coef(xr_ctx, cw, gw, gb, lam, tm_ctx)
            afl, bfl, arl, brl = _lru_coef(xr_lat, cw, gw, gb, lam, tm_lat)
            h0 = jnp.zeros((bsz, 1, d), F32)
            sfc, hfc = _scan(afc, bfc, h0, None, False, tm_ctx)
            r_ctx, hrc = _scan(arc, brc, h0, sfc, True, tm_ctx)
            sfl, _ = _scan(afl, bfl, hfc, None, False, tm_lat)
            r_lat, _ = _scan(arl, brl, hrc, sfl, True, tm_lat)
            o_lat, o_ctx = g_lat, (g_ctx if with_ctx_out else None)
            w_o = lru_w_out[idx].astype(BF16)

        cw_ffn = _pad_rows(jnp.concatenate([ffn_conv_w[layer], ffn_conv_b[layer][None, :]], axis=0), 8)
        wf = (ffn_w_up[layer].astype(BF16), cw_ffn, ffn_w_down[layer].astype(BF16))

        x = _block_tail(x, o_lat, r_lat, w_o, ml[2], n2, ml[3], ml[4], ml[5], wf, tm_lat)
        if with_ctx_out:
            ctx = _block_tail(ctx, o_ctx, r_ctx, w_o, mc[2], n2, mc[3], mc[4], mc[5], wf, tm_ctx)
    return x
```
